```python
import math
import jax, jax.numpy as jnp
from jax import lax
import numpy as np

D_MODEL = 1024
BATCH = 2
SEQ = 8192
DEPTH = 1

SSD_INNER = D_MODEL
SSD_HEADS = 16
SSD_HEAD_DIM = SSD_INNER // SSD_HEADS
SSD_GROUPS = 2
SSD_STATE = 128
SSD_CONV = 4
SSD_CHUNK = 128
SSD_XBC = SSD_INNER + 2 * SSD_GROUPS * SSD_STATE

SB_HEADS = 16
SB_HEAD_DIM = 64
SB_WIDTH = SB_HEADS * SB_HEAD_DIM
SB_BLOCK = 128

MIX_WIDTH = SSD_INNER + SB_WIDTH
IN_PROJ = SSD_INNER + SSD_XBC + SSD_HEADS + 3 * SB_WIDTH

MEM_TOKENS = 256
MEM_HEADS = 4
MEM_HEAD_DIM = D_MODEL // MEM_HEADS

D_FF = 2816
FFN_CONV = 3

EPS = 1e-6

kernel_name = "hymba_ssd_stickbreaking_hybrid_layer"


def rmsnorm(x, w):
    xf = x.astype(jnp.float32)
    y = xf * lax.rsqrt(jnp.mean(xf * xf, axis=-1, keepdims=True) + EPS)
    return y.astype(x.dtype) * w


def causal_dwconv(x, w, b):
    K, C = w.shape
    y = lax.conv_general_dilated(
        x, w[:, None, :].astype(x.dtype), window_strides=(1,), padding=[(K - 1, 0)],
        dimension_numbers=("NWC", "WIO", "NWC"), feature_group_count=C)
    return y + b


def ssd_chunked(xh, dt, A, Bh, Ch):
    b, l, h, p = xh.shape
    n = Bh.shape[-1]
    c = l // SSD_CHUNK
    L = SSD_CHUNK
    dtype = xh.dtype
    a = dt * A
    xc = (xh * dt[..., None].astype(dtype)).reshape(b, c, L, h, p)
    Bc = Bh.reshape(b, c, L, h, n)
    Cc = Ch.reshape(b, c, L, h, n)
    a_cs = jnp.cumsum(a.reshape(b, c, L, h).transpose(0, 3, 1, 2), axis=-1)
    seg = a_cs[..., :, None] - a_cs[..., None, :]
    tril = jnp.tril(jnp.ones((L, L), dtype=bool))
    Lmat = jnp.where(tril, jnp.exp(jnp.where(tril, seg, 0.0)), 0.0).astype(dtype)
    cb = jnp.einsum("bclhn,bcshn->bhcls", Cc, Bc) * Lmat
    y_diag = jnp.einsum("bhcls,bcshp->bclhp", cb, xc)
    decay_states = jnp.exp(a_cs[..., -1:] - a_cs).astype(dtype)
    states = jnp.einsum("bclhn,bhcl,bclhp->bchpn", Bc, decay_states, xc)
    chunk_decay = jnp.exp(a_cs[..., -1]).astype(dtype)

    def step(carry, inp):
        s, d = inp
        return carry * d[..., None, None] + s, carry

    init = jnp.zeros((b, h, p, n), dtype=states.dtype)
    _, prev = lax.scan(step, init, (states.transpose(1, 0, 2, 3, 4), chunk_decay.transpose(2, 0, 1)))
    prev = prev.transpose(1, 0, 2, 3, 4)
    y_off = jnp.einsum("bclhn,bchpn,bhcl->bclhp", Cc, prev, jnp.exp(a_cs).astype(dtype))
    return (y_diag + y_off).reshape(b, l, h, p)


def ssd_mixer(z, xbc, dt_raw, conv_w, conv_b, dt_bias, a_log, d_skip, norm_w):
    b, l, _ = xbc.shape
    xbc = jax.nn.silu(causal_dwconv(xbc, conv_w, conv_b))
    xs = xbc[..., :SSD_INNER]
    Bm = xbc[..., SSD_INNER:SSD_INNER + SSD_GROUPS * SSD_STATE].reshape(b, l, SSD_GROUPS, SSD_STATE)
    Cm = xbc[..., SSD_INNER + SSD_GROUPS * SSD_STATE:].reshape(b, l, SSD_GROUPS, SSD_STATE)
    rep = SSD_HEADS // SSD_GROUPS
    Bh = jnp.repeat(Bm, rep, axis=2)
    Ch = jnp.repeat(Cm, rep, axis=2)
    xh = xs.reshape(b, l, SSD_HEADS, SSD_HEAD_DIM)
    dt = jax.nn.softplus((dt_raw + dt_bias).astype(jnp.float32))
    A = -jnp.exp(a_log.astype(jnp.float32))
    y = ssd_chunked(xh, dt, A, Bh, Ch)
    y = y + xh * d_skip[:, None]
    y = y.reshape(b, l, SSD_INNER) * jax.nn.silu(z)
    yg = y.reshape(b, l, SSD_GROUPS, SSD_INNER // SSD_GROUPS)
    yg = rmsnorm(yg, norm_w.reshape(SSD_GROUPS, SSD_INNER // SSD_GROUPS))
    return yg.reshape(b, l, SSD_INNER)


def stick_breaking_attention(q, k, v, norm_w):
    b, S, _ = q.shape
    nb = S // SB_BLOCK
    qh = q.reshape(b, S, SB_HEADS, SB_HEAD_DIM).transpose(0, 2, 1, 3)
    kh = k.reshape(b, S, SB_HEADS, SB_HEAD_DIM).transpose(0, 2, 1, 3)
    vh = v.reshape(b, S, SB_HEADS, SB_HEAD_DIM).transpose(0, 2, 1, 3)
    qb = qh.reshape(b, SB_HEADS, nb, SB_BLOCK, SB_HEAD_DIM).transpose(2, 0, 1, 3, 4)
    starts = jnp.arange(nb, dtype=jnp.int32) * SB_BLOCK
    scale = 1.0 / math.sqrt(SB_HEAD_DIM)
    s_idx = jnp.arange(S, dtype=jnp.int32)

    def block(args):
        qblk, start = args
        zz = jnp.einsum("bhqd,bhkd->bhqk", qblk, kh).astype(jnp.float32) * scale
        t_idx = start + jnp.arange(SB_BLOCK, dtype=jnp.int32)
        causal = s_idx[None, :] < t_idx[:, None]
        lnb = jnp.where(causal, jax.nn.log_sigmoid(-zz), 0.0)
        excl = lax.cumsum(lnb, axis=3, reverse=True) - lnb
        A = jnp.where(causal, jnp.exp(jax.nn.log_sigmoid(zz) + excl), 0.0)
        return jnp.einsum("bhqk,bhkd->bhqd", A.astype(vh.dtype), vh)

    out = lax.map(block, (qb, starts))
    out = out.transpose(1, 0, 3, 2, 4).reshape(b, S, SB_HEADS, SB_HEAD_DIM)
    out = rmsnorm(out, norm_w.reshape(SB_HEADS, SB_HEAD_DIM))
    return out.reshape(b, S, SB_WIDTH)


def memory_cross_attention(h, m, w_q, w_k, w_v, w_o):
    b, S, _ = h.shape
    M = m.shape[1]
    q = (h @ w_q).reshape(b, S, MEM_HEADS, MEM_HEAD_DIM)
    k = (m @ w_k).reshape(b, M, MEM_HEADS, MEM_HEAD_DIM)
    v = (m @ w_v).reshape(b, M, MEM_HEADS, MEM_HEAD_DIM)
    sc = jnp.einsum("bshd,bmhd->bhsm", q, k).astype(jnp.float32) / math.sqrt(MEM_HEAD_DIM)
    p = jax.nn.softmax(sc, axis=-1).astype(v.dtype)
    o = jnp.einsum("bhsm,bmhd->bshd", p, v).reshape(b, S, D_MODEL)
    return o @ w_o


def conv_glu_ffn(h, w_up, conv_w, conv_b, w_down):
    u = causal_dwconv(h @ w_up, conv_w, conv_b)
    g, val = u[..., :D_FF], u[..., D_FF:]
    return (jax.nn.silu(g) * val) @ w_down


def setup_inputs(seed: int = 0) -> dict:
    key = jax.random.key(seed)
    ks = jax.random.split(key, 32)
    f32 = jnp.float32

    def nrm(k, shape, fan_in):
        return jax.random.normal(k, shape, f32) * (fan_in ** -0.5)

    def gain(k, shape):
        return 1.0 + 0.02 * jax.random.normal(k, shape, f32)

    Ld = DEPTH
    dt0 = jnp.exp(jax.random.uniform(ks[8], (Ld, SSD_HEADS), f32, math.log(1e-3), math.log(1e-1)))
    dt_bias = dt0 + jnp.log(-jnp.expm1(-dt0))
    return {
        "x": jax.random.normal(ks[0], (BATCH, SEQ, D_MODEL), f32),
        "mem": jax.random.normal(ks[1], (BATCH, MEM_TOKENS, D_MODEL), f32),
        "norm_mix_w": gain(ks[2], (Ld, D_MODEL)),
        "w_in": nrm(ks[3], (Ld, D_MODEL, IN_PROJ), D_MODEL),
        "conv_ssd_w": nrm(ks[4], (Ld, SSD_CONV, SSD_XBC), SSD_CONV),
        "conv_ssd_b": 0.02 * jax.random.normal(ks[5], (Ld, SSD_XBC), f32),
        "dt_bias": dt_bias,
        "a_log": jnp.log(jax.random.uniform(ks[6], (Ld, SSD_HEADS), f32, 1.0, 16.0)),
        "d_skip": gain(ks[7], (Ld, SSD_HEADS)),
        "ssd_norm_w": gain(ks[9], (Ld, SSD_INNER)),
        "sb_norm_w": gain(ks[10], (Ld, SB_WIDTH)),
        "w_out": nrm(ks[11], (Ld, MIX_WIDTH, D_MODEL), MIX_WIDTH),
        "norm_mem_w": gain(ks[12], (Ld, D_MODEL)),
        "norm_memkv_w": gain(ks[13], (Ld, D_MODEL)),
        "w_mq": nrm(ks[14], (Ld, D_MODEL, D_MODEL), D_MODEL),
        "w_mk": nrm(ks[15], (Ld, D_MODEL, D_MODEL), D_MODEL),
        "w_mv": nrm(ks[16], (Ld, D_MODEL, D_MODEL), D_MODEL),
        "w_mo": nrm(ks[17], (Ld, D_MODEL, D_MODEL), D_MODEL),
        "norm_ffn_w": gain(ks[18], (Ld, D_MODEL)),
        "w_up": nrm(ks[19], (Ld, D_MODEL, 2 * D_FF), D_MODEL),
        "conv_ffn_w": nrm(ks[20], (Ld, FFN_CONV, 2 * D_FF), FFN_CONV),
        "conv_ffn_b": 0.02 * jax.random.normal(ks[21], (Ld, 2 * D_FF), f32),
        "w_down": nrm(ks[22], (Ld, D_FF, D_MODEL), D_FF),
        "norm_final_w": gain(ks[23], (D_MODEL,)),
    }


def reference(x, mem, norm_mix_w, w_in, conv_ssd_w, conv_ssd_b, dt_bias, a_log, d_skip,
              ssd_norm_w, sb_norm_w, w_out, norm_mem_w, norm_memkv_w, w_mq, w_mk, w_mv, w_mo,
              norm_ffn_w, w_up, conv_ffn_w, conv_ffn_b, w_down, norm_final_w):
    o1 = SSD_INNER
    o2 = o1 + SSD_XBC
    o3 = o2 + SSD_HEADS
    o4 = o3 + SB_WIDTH
    o5 = o4 + SB_WIDTH
    for l in range(DEPTH):
        h = rmsnorm(x, norm_mix_w[l])
        proj = h @ w_in[l]
        z, xbc, dt_raw = proj[..., :o1], proj[..., o1:o2], proj[..., o2:o3]
        q, k, v = proj[..., o3:o4], proj[..., o4:o5], proj[..., o5:]
        y_ssd = ssd_mixer(z, xbc, dt_raw, conv_ssd_w[l], conv_ssd_b[l], dt_bias[l],
                          a_log[l], d_skip[l], ssd_norm_w[l])
        y_sb = stick_breaking_attention(q, k, v, sb_norm_w[l])
        x = x + jnp.concatenate([y_ssd, y_sb], axis=-1) @ w_out[l]
        h = rmsnorm(x, norm_mem_w[l])
        m = rmsnorm(mem, norm_memkv_w[l])
        x = x + memory_cross_attention(h, m, w_mq[l], w_mk[l], w_mv[l], w_mo[l])
        h = rmsnorm(x, norm_ffn_w[l])
        x = x + conv_glu_ffn(h, w_up[l], conv_ffn_w[l], conv_ffn_b[l], w_down[l])
    return rmsnorm(x, norm_final_w)
```

```python
import functools
import math

import jax
import jax.numpy as jnp
from jax import lax
from jax.experimental import pallas as pl
from jax.experimental.pallas import tpu as pltpu

EPS = 1e-6
LANES = 128
SUBLANES = 8
VMEM_LIMIT = 56 * 1024 * 1024

SSD_HEADS = 16
SSD_GROUPS = 2
SSD_STATE = 128
SSD_CONV = 4
SSD_CHUNK = 128
SB_HEAD_DIM = 64
SB_TILE = 128
MEM_HEADS = 4
FFN_CONV = 3
FFN_CHUNK = 256

SB_SKIP_LOG = -110.0

_BF = jnp.bfloat16
_F32 = jnp.float32


def _dot(a, b):
    return jnp.dot(a, b, preferred_element_type=_F32)


def _dot_nt(a, b):
    return lax.dot_general(a, b, (((1,), (1,)), ((), ())), preferred_element_type=_F32)


def _rms(x, w):
    return x * lax.rsqrt(jnp.mean(x * x, axis=-1, keepdims=True) + EPS) * w


def _softplus(x):
    return jnp.maximum(x, 0.0) + jnp.log(1.0 + jnp.exp(-jnp.abs(x)))


def _silu(x):
    return x * jax.nn.sigmoid(x)


def _params(*sem):
    return pltpu.CompilerParams(dimension_semantics=sem, vmem_limit_bytes=VMEM_LIMIT)


def _const_spec(shape):
    nd = len(shape)
    return pl.BlockSpec(shape, lambda *_: (0,) * nd)


def _inproj_kernel(x_ref, nw_ref, wz_ref, wxbc_ref, wdt_ref, wq_ref, wk_ref, wv_ref,
                   z_ref, xbc_ref, dt_ref, q_ref, k_ref, v_ref, *, q_scale):
    hb = _rms(x_ref[...], nw_ref[...]).astype(_BF)
    z_ref[...] = _dot(hb, wz_ref[...])
    xbc_ref[...] = _dot(hb, wxbc_ref[...])
    dt_ref[...] = _dot(hb, wdt_ref[...])
    q_ref[...] = (_dot(hb, wq_ref[...]) * q_scale).astype(_BF)
    k_ref[...] = _dot(hb, wk_ref[...]).astype(_BF)
    v_ref[...] = _dot(hb, wv_ref[...]).astype(_BF)


def _in_proj(x2d, nw, wz, wxbc, wdt, wq, wk, wv, *, tm):
    T, D = x2d.shape
    row = lambda n: pl.BlockSpec((tm, n), lambda i: (i, 0))
    ws = (wz, wxbc, wdt, wq, wk, wv)
    out_dtypes = (_F32, _F32, _F32, _BF, _BF, _BF)
    return pl.pallas_call(
        functools.partial(_inproj_kernel, q_scale=1.0 / math.sqrt(SB_HEAD_DIM)),
        grid=(T // tm,),
        in_specs=[row(D), _const_spec(nw.shape)] + [_const_spec(w.shape) for w in ws],
        out_specs=[row(w.shape[1]) for w in ws],
        out_shape=[jax.ShapeDtypeStruct((T, w.shape[1]), dt) for w, dt in zip(ws, out_dtypes)],
        compiler_params=_params("arbitrary"),
        name="in_proj",
    )(x2d, nw, *ws)


def _split3(v):
    v1 = v.astype(_BF)
    r1 = v - v1.astype(_F32)
    v2 = r1.astype(_BF)
    v3 = (r1 - v2.astype(_F32)).astype(_BF)
    return v1, v2, v3


def _ssd_kernel(z_ref, xbc_ref, dt_ref, cw_ref, cb_ref, dtb_ref, alog_ref, dsk_ref, nw_ref,
                y_ref, ext_ref, state_ref, *, L, H, P, G, N):
    DI = H * P
    HG = H // G
    GW = DI // G
    c = pl.program_id(1)

    @pl.when(c == 0)
    def _():
        ext_ref[0:SUBLANES, :] = jnp.zeros((SUBLANES, ext_ref.shape[1]), _F32)
        state_ref[...] = jnp.zeros_like(state_ref)

    ext_ref[SUBLANES:SUBLANES + L, :] = xbc_ref[...]
    conv = cb_ref[...]
    for kk in range(SSD_CONV):
        off = SUBLANES - (SSD_CONV - 1) + kk
        conv = conv + cw_ref[kk:kk + 1, :] * ext_ref[off:off + L, :]
    ext_ref[0:SUBLANES, :] = ext_ref[L:L + SUBLANES, :]
    act = _silu(conv)
    xs = act[:, :DI]
    Bm = act[:, DI:DI + G * N]
    Cm = act[:, DI + G * N:]

    dt = _softplus(dt_ref[...] + dtb_ref[...])
    a = dt * (-jnp.exp(alog_ref[...]))
    ri = lax.broadcasted_iota(jnp.int32, (L, L), 0)
    ci = lax.broadcasted_iota(jnp.int32, (L, L), 1)
    tril = ri >= ci
    tril_bf = jnp.where(tril, 1.0, 0.0).astype(_BF)
    a1, a2, a3 = _split3(a)
    a_cs = _dot(tril_bf, a1) + _dot(tril_bf, a2) + _dot(tril_bf, a3)
    a_last = a_cs[L - 1:L, :]
    a_cs_t = a_cs.T
    dt_t = dt.T
    w_col = jnp.exp(a_last - a_cs) * dt
    e_col = jnp.exp(a_cs)
    c_dec = jnp.exp(a_last)

    lane = lax.broadcasted_iota(jnp.int32, (1, 2 * P), 1)
    lo = lane < P

    for g in range(G):
        B_g = Bm[:, g * N:(g + 1) * N]
        C_g = Cm[:, g * N:(g + 1) * N].astype(_BF)
        cb = _dot_nt(C_g, B_g.astype(_BF))
        b_t = B_g.T.astype(_BF)
        ys = []
        for pr in range(HG // 2):
            h0 = g * HG + 2 * pr
            h1 = h0 + 1
            cols = slice(h0 * P, (h0 + 2) * P)
            xs2 = xs[:, cols]
            pick = lambda v: jnp.where(lo, v[:, h0:h0 + 1], v[:, h1:h1 + 1])
            prev2 = state_ref[:, cols]
            y_off = _dot(C_g, prev2.astype(_BF)) * pick(e_col)
            ms = []
            for hh in (h0, h1):
                seg = a_cs[:, hh:hh + 1] - a_cs_t[hh:hh + 1, :]
                lmat = jnp.where(tril, jnp.exp(jnp.where(tril, seg, 0.0)), 0.0)
                ms.append((cb * lmat * dt_t[hh:hh + 1, :]).astype(_BF))
            xs_lo = jnp.where(lo, xs2, 0.0).astype(_BF)
            xs_hi = jnp.where(lo, 0.0, xs2).astype(_BF)
            y_diag = _dot(jnp.concatenate(ms, axis=1), jnp.concatenate([xs_lo, xs_hi], axis=0))
            new_state = _dot(b_t, (xs2 * pick(w_col)).astype(_BF))
            state_ref[:, cols] = prev2 * pick(c_dec) + new_state
            y2 = y_diag + y_off + xs2 * dsk_ref[:, cols]
            ys.append(y2 * _silu(z_ref[:, cols]))
        yg = jnp.concatenate(ys, axis=1)
        y_ref[:, g * GW:(g + 1) * GW] = _rms(yg, nw_ref[:, g * GW:(g + 1) * GW]).astype(_BF)


def _ssd(z, xbc, dt_raw, cw, cb, dtb, alog, dsk, nw, *, B, S):
    T, DI = z.shape
    C = xbc.shape[1]
    L = SSD_CHUNK
    H, G, N = SSD_HEADS, SSD_GROUPS, SSD_STATE
    P = DI // H
    assert 2 * P == LANES and H % (2 * G) == 0 and S % L == 0
    nc = S // L
    row = lambda n: pl.BlockSpec((L, n), lambda b, c: (b * nc + c, 0))
    consts = (cw, cb, dtb, alog, dsk, nw)
    return pl.pallas_call(
        functools.partial(_ssd_kernel, L=L, H=H, P=P, G=G, N=N),
        grid=(B, nc),
        in_specs=[row(DI), row(C), row(LANES)] + [_const_spec(a.shape) for a in consts],
        out_specs=row(DI),
        out_shape=jax.ShapeDtypeStruct((T, DI), _BF),
        scratch_shapes=[pltpu.VMEM((L + SUBLANES, C), _F32), pltpu.VMEM((N, DI), _F32)],
        compiler_params=_params("arbitrary", "arbitrary"),
        name="ssd",
    )(z, xbc, dt_raw, *consts)


def _sb_kernel(q_ref, k_ref, v_ref, nw_ref, o_ref, acc_ref, r_ref, *, tq, hd):
    i = pl.program_id(2)
    lane = lax.broadcasted_iota(jnp.int32, (1, 2 * hd), 1)
    lo = lane < hd
    q2 = q_ref[...]
    zq = jnp.zeros_like(q2)
    qs = (jnp.where(lo, q2, zq), jnp.where(lo, zq, q2))
    ri = lax.broadcasted_iota(jnp.int32, (tq, tq), 0)
    ci = lax.broadcasted_iota(jnp.int32, (tq, tq), 1)
    tri = jnp.where(ri >= ci, 1.0, 0.0).astype(_BF)
    tri2 = jnp.concatenate([tri, tri], axis=0)
    causal = ci < ri

    acc_ref[...] = jnp.zeros_like(acc_ref)
    r_ref[...] = jnp.zeros_like(r_ref)

    def tile(j, diag):
        start = pl.multiple_of(j * tq, tq)
        k2 = k_ref[pl.ds(start, tq), :]
        v2 = v_ref[pl.ds(start, tq), :]
        zv = jnp.zeros_like(v2)
        vv = jnp.concatenate([jnp.where(lo, v2, zv), jnp.where(lo, zv, v2)], axis=0)
        ps = []
        for idx in range(2):
            z = _dot_nt(qs[idx], k2)
            lnb = -_softplus(z)
            if diag:
                lnb = jnp.where(causal, lnb, 0.0)
            hi = lnb.astype(_BF)
            lw = (lnb - hi.astype(_F32)).astype(_BF)
            cum = _dot(jnp.concatenate([hi, lw], axis=1), tri2)
            r = r_ref[idx]
            a = jnp.exp(z + cum + r)
            if diag:
                a = jnp.where(causal, a, 0.0)
            r_ref[idx] = r + cum[:, 0:1]
            ps.append(a.astype(_BF))
        acc_ref[...] += _dot(jnp.concatenate(ps, axis=1), vv)

    tile(i, True)

    def alive():
        return (jnp.max(jnp.maximum(r_ref[0], r_ref[1])) > SB_SKIP_LOG).astype(jnp.int32)

    def cond(carry):
        j, live = carry
        return jnp.logical_and(j >= 0, live > 0)

    def body(carry):
        j, _ = carry
        tile(j, False)
        return j - 1, alive()

    lax.while_loop(cond, body, (i - 1, alive()))

    acc = acc_ref[...]
    sq = acc * acc
    s_lo = jnp.sum(jnp.where(lo, sq, 0.0), axis=-1, keepdims=True)
    s_hi = jnp.sum(jnp.where(lo, 0.0, sq), axis=-1, keepdims=True)
    inv = lax.rsqrt(jnp.where(lo, s_lo, s_hi) * (1.0 / hd) + EPS)
    o_ref[...] = (acc * inv * nw_ref[...]).astype(_BF)


def _sb_attention(q, k, v, nw, *, B, S):
    T, W = q.shape
    hd = SB_HEAD_DIM
    tq = SB_TILE
    assert 2 * hd == LANES and tq == LANES and S % tq == 0
    nq = S // tq
    npair = W // LANES
    k3 = k.reshape(B, S, W)
    v3 = v.reshape(B, S, W)
    return pl.pallas_call(
        functools.partial(_sb_kernel, tq=tq, hd=hd),
        grid=(B, npair, nq),
        in_specs=[
            pl.BlockSpec((tq, LANES), lambda b, p, i: (b * nq + i, p)),
            pl.BlockSpec((None, S, LANES), lambda b, p, i: (b, 0, p)),
            pl.BlockSpec((None, S, LANES), lambda b, p, i: (b, 0, p)),
            pl.BlockSpec((1, LANES), lambda b, p, i: (0, p)),
        ],
        out_specs=pl.BlockSpec((tq, LANES), lambda b, p, i: (b * nq + i, p)),
        out_shape=jax.ShapeDtypeStruct((T, W), _BF),
        scratch_shapes=[pltpu.VMEM((tq, LANES), _F32), pltpu.VMEM((2, tq, tq), _F32)],
        compiler_params=_params("arbitrary", "arbitrary", "arbitrary"),
        name="sb_attn",
    )(q, k3, v3, nw)


def _memkv_kernel(m_ref, nw_ref, wk_ref, wv_ref, k_ref, v_ref):
    mb = _rms(m_ref[...], nw_ref[...]).astype(_BF)
    k_ref[...] = _dot(mb, wk_ref[...]).astype(_BF)
    v_ref[...] = _dot(mb, wv_ref[...]).astype(_BF)


def _mem_kv(mem2d, nw, wk, wv):
    R, D = mem2d.shape
    return pl.pallas_call(
        _memkv_kernel,
        grid=(1,),
        in_specs=[_const_spec(a.shape) for a in (mem2d, nw, wk, wv)],
        out_specs=[_const_spec((R, D))] * 2,
        out_shape=[jax.ShapeDtypeStruct((R, D), _BF)] * 2,
        compiler_params=_params("arbitrary"),
        name="mem_kv",
    )(mem2d, nw, wk, wv)


def _mixmem_kernel(x_ref, ya_ref, yb_ref, wo1_ref, wo2_ref, nw_ref, wq_ref, mk_ref, mv_ref, wo_ref,
                   o_ref, *, heads, q_scale):
    x1 = x_ref[...] + _dot(ya_ref[...], wo1_ref[...]) + _dot(yb_ref[...], wo2_ref[...])
    hb = _rms(x1, nw_ref[...]).astype(_BF)
    q = (_dot(hb, wq_ref[...]) * q_scale).astype(_BF)
    D = q.shape[1]
    hd = D // heads
    outs = []
    for h in range(heads):
        cols = slice(h * hd, (h + 1) * hd)
        sc = _dot_nt(q[:, cols], mk_ref[:, cols])
        sc = sc - jnp.max(sc, axis=-1, keepdims=True)
        e = jnp.exp(sc)
        p = e / jnp.sum(e, axis=-1, keepdims=True)
        outs.append(_dot(p.astype(_BF), mv_ref[:, cols]).astype(_BF))
    o = jnp.concatenate(outs, axis=1)
    o_ref[...] = x1 + _dot(o, wo_ref[...])


def _mix_mem(x2d, ya, yb, wo1, wo2, nw, wq, mk, mv, wo, *, B, S, tm):
    T, D = x2d.shape
    M = mk.shape[0] // B
    nt = S // tm
    row = pl.BlockSpec((tm, D), lambda b, i: (b * nt + i, 0))
    memspec = pl.BlockSpec((M, D), lambda b, i: (b, 0))
    hd = D // MEM_HEADS
    return pl.pallas_call(
        functools.partial(_mixmem_kernel, heads=MEM_HEADS, q_scale=1.0 / math.sqrt(hd)),
        grid=(B, nt),
        in_specs=[row, row, row, _const_spec(wo1.shape), _const_spec(wo2.shape), _const_spec(nw.shape),
                  _const_spec(wq.shape), memspec, memspec, _const_spec(wo.shape)],
        out_specs=row,
        out_shape=jax.ShapeDtypeStruct((T, D), _F32),
        compiler_params=_params("arbitrary", "arbitrary"),
        name="mix_mem",
    )(x2d, ya, yb, wo1, wo2, nw, wq, mk, mv, wo)


def _ffn_kernel(x_ref, halo_ref, nw_ref, wup_ref, cw_ref, cb_ref, wdn_ref, fw_ref,
                o_ref, acc_ref, ug_ref, uv_ref, *, tm, nchunk, final_norm):
    i = pl.program_id(1)
    x = x_ref[...]
    hb = _rms(x, nw_ref[...]).astype(_BF)
    keep = (i > 0).astype(_F32)
    hh = (_rms(halo_ref[...], nw_ref[...]) * keep).astype(_BF)
    acc_ref[...] = jnp.zeros_like(acc_ref)

    def conv(u_ref, w_main, w_halo, cw, cb):
        u_ref[0:SUBLANES, :] = _dot(hh, w_halo)
        u_ref[SUBLANES:SUBLANES + tm, :] = _dot(hb, w_main)
        out = cb
        for kk in range(FFN_CONV):
            off = SUBLANES - (FFN_CONV - 1) + kk
            out = out + cw[kk:kk + 1, :] * u_ref[off:off + tm, :]
        return out

    def chunk(c, carry):
        wg = wup_ref[c]
        wv = wup_ref[nchunk + c]
        g = conv(ug_ref, wg, wg, cw_ref[c], cb_ref[c])
        val = conv(uv_ref, wv, wv, cw_ref[nchunk + c], cb_ref[nchunk + c])
        acc_ref[...] += _dot((_silu(g) * val).astype(_BF), wdn_ref[c])
        return carry

    lax.fori_loop(0, nchunk, chunk, 0)
    y = x + acc_ref[...]
    o_ref[...] = _rms(y, fw_ref[...]) if final_norm else y


def _ffn(x2d, nw, wup, cw, cb, wdn, fw, *, B, S, tm, final_norm):
    T, D = x2d.shape
    nchunk, fc, _ = wdn.shape
    nt = S // tm
    hb = tm // SUBLANES
    row = pl.BlockSpec((tm, D), lambda b, i: (b * nt + i, 0))
    halo = pl.BlockSpec((SUBLANES, D), lambda b, i: (jnp.maximum((b * nt + i) * hb - 1, 0), 0))
    return pl.pallas_call(
        functools.partial(_ffn_kernel, tm=tm, nchunk=nchunk, final_norm=final_norm),
        grid=(B, nt),
        in_specs=[row, halo] + [_const_spec(a.shape) for a in (nw, wup, cw, cb, wdn, fw)],
        out_specs=row,
        out_shape=jax.ShapeDtypeStruct((T, D), _F32),
        scratch_shapes=[pltpu.VMEM((tm, D), _F32),
                        pltpu.VMEM((tm + SUBLANES, fc), _F32),
                        pltpu.VMEM((tm + SUBLANES, fc), _F32)],
        compiler_params=_params("arbitrary", "arbitrary"),
        name="ffn",
    )(x2d, x2d, nw, wup, cw, cb, wdn, fw)


def _pad_lanes(a):
    return jnp.pad(a, ((0, 0), (0, LANES - a.shape[1])))


def kernel(x, mem, norm_mix_w, w_in, conv_ssd_w, conv_ssd_b, dt_bias, a_log, d_skip, ssd_norm_w, sb_norm_w, w_out, norm_mem_w, norm_memkv_w, w_mq, w_mk, w_mv, w_mo, norm_ffn_w, w_up, conv_ffn_w, conv_ffn_b, w_down, norm_final_w):
    B, S, D = x.shape
    T = B * S
    depth = w_in.shape[0]
    H = SSD_HEADS
    DI = ssd_norm_w.shape[1]
    C = conv_ssd_w.shape[2]
    W = sb_norm_w.shape[1]
    DFF = w_down.shape[1]
    o1, o2 = DI, DI + C
    o3 = o2 + H
    o4, o5 = o3 + W, o3 + 2 * W
    fc = FFN_CHUNK
    nchunk = DFF // fc
    assert DFF % fc == 0
    tm = min(512, S)

    x2d = x.reshape(T, D)
    mem2d = mem.reshape(B * mem.shape[1], D)
    for l in range(depth):
        wi = w_in[l]
        bf = lambda a: a.astype(_BF)
        z, xbc, dt_raw, q, k, v = _in_proj(
            x2d, norm_mix_w[l][None], bf(wi[:, :o1]), bf(wi[:, o1:o2]), bf(_pad_lanes(wi[:, o2:o3])),
            bf(wi[:, o3:o4]), bf(wi[:, o4:o5]), bf(wi[:, o5:]), tm=min(256, S))
        y_ssd = _ssd(z, xbc, dt_raw, conv_ssd_w[l], conv_ssd_b[l][None], _pad_lanes(dt_bias[l][None]),
                     _pad_lanes(a_log[l][None]), jnp.repeat(d_skip[l], DI // H)[None], ssd_norm_w[l][None],
                     B=B, S=S)
        y_sb = _sb_attention(q, k, v, sb_norm_w[l][None], B=B, S=S)
        mk, mv = _mem_kv(mem2d, norm_memkv_w[l][None], bf(w_mk[l]), bf(w_mv[l]))
        x2d = _mix_mem(x2d, y_ssd, y_sb, bf(w_out[l][:DI]), bf(w_out[l][DI:]), norm_mem_w[l][None],
                       bf(w_mq[l]), mk, mv, bf(w_mo[l]), B=B, S=S, tm=tm)
        wup = bf(w_up[l]).reshape(D, 2 * nchunk, fc).transpose(1, 0, 2)
        cw = conv_ffn_w[l].reshape(FFN_CONV, 2 * nchunk, fc).transpose(1, 0, 2)
        cb = conv_ffn_b[l].reshape(2 * nchunk, 1, fc)
        x2d = _ffn(x2d, norm_ffn_w[l][None], wup, cw, cb, bf(w_down[l]).reshape(nchunk, fc, D),
                   norm_final_w[None], B=B, S=S, tm=tm, final_norm=(l == depth - 1))
    return x2d.reshape(B, S, D)
```

```python
import functools
import math

import jax
import jax.numpy as jnp
from jax import lax
from jax.experimental import pallas as pl
from jax.experimental.pallas import tpu as pltpu

EPS = 1e-6
LANES = 128
SUBLANES = 8
VMEM_LIMIT = 56 * 1024 * 1024

SSD_HEADS = 16
SSD_GROUPS = 2
SSD_STATE = 128
SSD_CONV = 4
SSD_CHUNK = 128
SB_HEAD_DIM = 64
SB_TILE = 128
SB_PAIRS_PER_STEP = 4
MEM_HEADS = 4
FFN_CONV = 3
FFN_CHUNK = 256

SB_SKIP_LOG = -110.0

_BF = jnp.bfloat16
_F32 = jnp.float32


def _dot(a, b):
    return jnp.dot(a, b, preferred_element_type=_F32)


def _dot_nt(a, b):
    return lax.dot_general(a, b, (((1,), (1,)), ((), ())), preferred_element_type=_F32)


def _rms(x, w):
    return x * lax.rsqrt(jnp.mean(x * x, axis=-1, keepdims=True) + EPS) * w


def _softplus(x):
    return jnp.maximum(x, 0.0) + jnp.log(1.0 + jnp.exp(-jnp.abs(x)))


def _silu(x):
    return x * jax.nn.sigmoid(x)


def _params(*sem):
    return pltpu.CompilerParams(dimension_semantics=sem, vmem_limit_bytes=VMEM_LIMIT)


def _const_spec(shape):
    nd = len(shape)
    return pl.BlockSpec(shape, lambda *_: (0,) * nd)


def _inproj_kernel(x_ref, nw_ref, wz_ref, wxbc_ref, wdt_ref, wq_ref, wk_ref, wv_ref,
                   z_ref, xbc_ref, dt_ref, q_ref, k_ref, v_ref, *, q_scale):
    hb = _rms(x_ref[...], nw_ref[...]).astype(_BF)
    z_ref[...] = _dot(hb, wz_ref[...])
    xbc_ref[...] = _dot(hb, wxbc_ref[...])
    dt_ref[...] = _dot(hb, wdt_ref[...])
    q_ref[...] = (_dot(hb, wq_ref[...]) * q_scale).astype(_BF)
    k_ref[...] = _dot(hb, wk_ref[...]).astype(_BF)
    v_ref[...] = _dot(hb, wv_ref[...]).astype(_BF)


def _in_proj(x2d, nw, wz, wxbc, wdt, wq, wk, wv, *, tm):
    T, D = x2d.shape
    row = lambda n: pl.BlockSpec((tm, n), lambda i: (i, 0))
    ws = (wz, wxbc, wdt, wq, wk, wv)
    out_dtypes = (_F32, _F32, _F32, _BF, _BF, _BF)
    return pl.pallas_call(
        functools.partial(_inproj_kernel, q_scale=1.0 / math.sqrt(SB_HEAD_DIM)),
        grid=(T // tm,),
        in_specs=[row(D), _const_spec(nw.shape)] + [_const_spec(w.shape) for w in ws],
        out_specs=[row(w.shape[1]) for w in ws],
        out_shape=[jax.ShapeDtypeStruct((T, w.shape[1]), dt) for w, dt in zip(ws, out_dtypes)],
        compiler_params=_params("arbitrary"),
        name="in_proj",
    )(x2d, nw, *ws)


def _split3(v):
    v1 = v.astype(_BF)
    r1 = v - v1.astype(_F32)
    v2 = r1.astype(_BF)
    v3 = (r1 - v2.astype(_F32)).astype(_BF)
    return v1, v2, v3


def _ssd_kernel(z_ref, xbc_ref, dt_ref, cw_ref, cb_ref, dtb_ref, alog_ref, dsk_ref, nw_ref,
                y_ref, ext_ref, state_ref, *, L, H, P, G, N):
    DI = H * P
    HG = H // G
    GW = DI // G
    c = pl.program_id(1)

    @pl.when(c == 0)
    def _():
        ext_ref[0:SUBLANES, :] = jnp.zeros((SUBLANES, ext_ref.shape[1]), _F32)
        state_ref[...] = jnp.zeros_like(state_ref)

    ext_ref[SUBLANES:SUBLANES + L, :] = xbc_ref[...]
    conv = cb_ref[...]
    for kk in range(SSD_CONV):
        off = SUBLANES - (SSD_CONV - 1) + kk
        conv = conv + cw_ref[kk:kk + 1, :] * ext_ref[off:off + L, :]
    ext_ref[0:SUBLANES, :] = ext_ref[L:L + SUBLANES, :]
    act = _silu(conv)
    xs = act[:, :DI]
    Bm = act[:, DI:DI + G * N]
    Cm = act[:, DI + G * N:]

    dt = _softplus(dt_ref[...] + dtb_ref[...])
    a = dt * (-jnp.exp(alog_ref[...]))
    ri = lax.broadcasted_iota(jnp.int32, (L, L), 0)
    ci = lax.broadcasted_iota(jnp.int32, (L, L), 1)
    tril = ri >= ci
    tril_bf = jnp.where(tril, 1.0, 0.0).astype(_BF)
    a1, a2, a3 = _split3(a)
    a_cs = _dot(tril_bf, a1) + _dot(tril_bf, a2) + _dot(tril_bf, a3)
    a_last = a_cs[L - 1:L, :]
    a_cs_t = a_cs.T
    dt_t = dt.T
    w_col = jnp.exp(a_last - a_cs) * dt
    e_col = jnp.exp(a_cs)
    c_dec = jnp.exp(a_last)

    lane = lax.broadcasted_iota(jnp.int32, (1, 2 * P), 1)
    lo = lane < P

    for g in range(G):
        B_g = Bm[:, g * N:(g + 1) * N]
        C_g = Cm[:, g * N:(g + 1) * N].astype(_BF)
        cb = _dot_nt(C_g, B_g.astype(_BF))
        b_t = B_g.T.astype(_BF)
        ys = []
        for pr in range(HG // 2):
            h0 = g * HG + 2 * pr
            h1 = h0 + 1
            cols = slice(h0 * P, (h0 + 2) * P)
            xs2 = xs[:, cols]
            pick = lambda v: jnp.where(lo, v[:, h0:h0 + 1], v[:, h1:h1 + 1])
            prev2 = state_ref[:, cols]
            y_off = _dot(C_g, prev2.astype(_BF)) * pick(e_col)
            ms = []
            for hh in (h0, h1):
                seg = a_cs[:, hh:hh + 1] - a_cs_t[hh:hh + 1, :]
                lmat = jnp.where(tril, jnp.exp(jnp.where(tril, seg, 0.0)), 0.0)
                ms.append((cb * lmat * dt_t[hh:hh + 1, :]).astype(_BF))
            xs_lo = jnp.where(lo, xs2, 0.0).astype(_BF)
            xs_hi = jnp.where(lo, 0.0, xs2).astype(_BF)
            y_diag = _dot(jnp.concatenate(ms, axis=1), jnp.concatenate([xs_lo, xs_hi], axis=0))
            new_state = _dot(b_t, (xs2 * pick(w_col)).astype(_BF))
            state_ref[:, cols] = prev2 * pick(c_dec) + new_state
            y2 = y_diag + y_off + xs2 * dsk_ref[:, cols]
            ys.append(y2 * _silu(z_ref[:, cols]))
        yg = jnp.concatenate(ys, axis=1)
        y_ref[:, g * GW:(g + 1) * GW] = _rms(yg, nw_ref[:, g * GW:(g + 1) * GW]).astype(_BF)


def _ssd(z, xbc, dt_raw, cw, cb, dtb, alog, dsk, nw, *, B, S):
    T, DI = z.shape
    C = xbc.shape[1]
    L = SSD_CHUNK
    H, G, N = SSD_HEADS, SSD_GROUPS, SSD_STATE
    P = DI // H
    assert 2 * P == LANES and H % (2 * G) == 0 and S % L == 0
    nc = S // L
    row = lambda n: pl.BlockSpec((L, n), lambda b, c: (b * nc + c, 0))
    consts = (cw, cb, dtb, alog, dsk, nw)
    return pl.pallas_call(
        functools.partial(_ssd_kernel, L=L, H=H, P=P, G=G, N=N),
        grid=(B, nc),
        in_specs=[row(DI), row(C), row(LANES)] + [_const_spec(a.shape) for a in consts],
        out_specs=row(DI),
        out_shape=jax.ShapeDtypeStruct((T, DI), _BF),
        scratch_shapes=[pltpu.VMEM((L + SUBLANES, C), _F32), pltpu.VMEM((N, DI), _F32)],
        compiler_params=_params("arbitrary", "arbitrary"),
        name="ssd",
    )(z, xbc, dt_raw, *consts)


def _sb_kernel(q_ref, k_ref, v_ref, nw_ref, o_ref, acc_ref, r_ref, *, tq, hd, npp):
    i = pl.program_id(2)
    lane = lax.broadcasted_iota(jnp.int32, (1, 2 * hd), 1)
    lo = lane < hd
    nh = 2 * npp
    qcat = []
    for p in range(npp):
        q2 = q_ref[:, p * LANES:(p + 1) * LANES]
        zq = jnp.zeros_like(q2)
        qcat.append(jnp.concatenate([jnp.where(lo, q2, zq), jnp.where(lo, zq, q2)], axis=0))
    ri = lax.broadcasted_iota(jnp.int32, (tq, tq), 0)
    ci = lax.broadcasted_iota(jnp.int32, (tq, tq), 1)
    tri = jnp.where(ri >= ci, 1.0, 0.0).astype(_BF)
    tri2 = jnp.concatenate([tri, tri], axis=0)
    rs = lax.broadcasted_iota(jnp.int32, (nh * tq, tq), 0) & (tq - 1)
    cs = lax.broadcasted_iota(jnp.int32, (nh * tq, tq), 1)
    causal = cs < rs

    acc_ref[...] = jnp.zeros_like(acc_ref)
    r_ref[...] = jnp.zeros_like(r_ref)

    def tile(j, diag):
        start = pl.multiple_of(j * tq, tq)
        z = jnp.concatenate(
            [_dot_nt(qcat[p], k_ref[pl.ds(start, tq), p * LANES:(p + 1) * LANES]) for p in range(npp)],
            axis=0)
        lnb = -_softplus(z)
        if diag:
            lnb = jnp.where(causal, lnb, 0.0)
        hi = lnb.astype(_BF)
        lw = (lnb - hi.astype(_F32)).astype(_BF)
        cum = _dot(jnp.concatenate([hi, lw], axis=1), tri2)
        r = r_ref[...]
        a = jnp.exp(z + cum + r)
        if diag:
            a = jnp.where(causal, a, 0.0)
        r_ref[...] = r + cum[:, 0:1]
        a = a.astype(_BF)
        for p in range(npp):
            cols = slice(p * LANES, (p + 1) * LANES)
            v2 = v_ref[pl.ds(start, tq), cols]
            zv = jnp.zeros_like(v2)
            vv = jnp.concatenate([jnp.where(lo, v2, zv), jnp.where(lo, zv, v2)], axis=0)
            pp = jnp.concatenate([a[2 * p * tq:(2 * p + 1) * tq], a[(2 * p + 1) * tq:(2 * p + 2) * tq]],
                                 axis=1)
            acc_ref[:, cols] += _dot(pp, vv)

    tile(i, True)

    def alive():
        return (jnp.max(r_ref[...]) > SB_SKIP_LOG).astype(jnp.int32)

    def cond(carry):
        j, live = carry
        return jnp.logical_and(j >= 0, live > 0)

    def body(carry):
        j, _ = carry
        tile(j, False)
        return j - 1, alive()

    lax.while_loop(cond, body, (i - 1, alive()))

    for p in range(npp):
        cols = slice(p * LANES, (p + 1) * LANES)
        acc = acc_ref[:, cols]
        sq = acc * acc
        s_lo = jnp.sum(jnp.where(lo, sq, 0.0), axis=-1, keepdims=True)
        s_hi = jnp.sum(jnp.where(lo, 0.0, sq), axis=-1, keepdims=True)
        inv = lax.rsqrt(jnp.where(lo, s_lo, s_hi) * (1.0 / hd) + EPS)
        o_ref[:, cols] = (acc * inv * nw_ref[:, cols]).astype(_BF)


def _sb_attention(q, k, v, nw, *, B, S):
    T, W = q.shape
    hd = SB_HEAD_DIM
    tq = SB_TILE
    npp = SB_PAIRS_PER_STEP
    cw = npp * LANES
    assert 2 * hd == LANES and tq == LANES and S % tq == 0 and W % cw == 0
    nq = S // tq
    k3 = k.reshape(B, S, W)
    v3 = v.reshape(B, S, W)
    return pl.pallas_call(
        functools.partial(_sb_kernel, tq=tq, hd=hd, npp=npp),
        grid=(B, W // cw, nq),
        in_specs=[
            pl.BlockSpec((tq, cw), lambda b, p, i: (b * nq + i, p)),
            pl.BlockSpec((None, S, cw), lambda b, p, i: (b, 0, p)),
            pl.BlockSpec((None, S, cw), lambda b, p, i: (b, 0, p)),
            pl.BlockSpec((1, cw), lambda b, p, i: (0, p)),
        ],
        out_specs=pl.BlockSpec((tq, cw), lambda b, p, i: (b * nq + i, p)),
        out_shape=jax.ShapeDtypeStruct((T, W), _BF),
        scratch_shapes=[pltpu.VMEM((tq, cw), _F32), pltpu.VMEM((2 * npp * tq, tq), _F32)],
        compiler_params=_params("arbitrary", "arbitrary", "arbitrary"),
        name="sb_attn",
    )(q, k3, v3, nw)


def _memkv_kernel(m_ref, nw_ref, wk_ref, wv_ref, k_ref, v_ref):
    mb = _rms(m_ref[...], nw_ref[...]).astype(_BF)
    k_ref[...] = _dot(mb, wk_ref[...]).astype(_BF)
    v_ref[...] = _dot(mb, wv_ref[...]).astype(_BF)


def _mem_kv(mem2d, nw, wk, wv):
    R, D = mem2d.shape
    return pl.pallas_call(
        _memkv_kernel,
        grid=(1,),
        in_specs=[_const_spec(a.shape) for a in (mem2d, nw, wk, wv)],
        out_specs=[_const_spec((R, D))] * 2,
        out_shape=[jax.ShapeDtypeStruct((R, D), _BF)] * 2,
        compiler_params=_params("arbitrary"),
        name="mem_kv",
    )(mem2d, nw, wk, wv)


def _mixmem_kernel(x_ref, ya_ref, yb_ref, wo1_ref, wo2_ref, nw_ref, wq_ref, mk_ref, mv_ref, wo_ref,
                   o_ref, *, heads, q_scale):
    x1 = x_ref[...] + _dot(ya_ref[...], wo1_ref[...]) + _dot(yb_ref[...], wo2_ref[...])
    hb = _rms(x1, nw_ref[...]).astype(_BF)
    q = (_dot(hb, wq_ref[...]) * q_scale).astype(_BF)
    D = q.shape[1]
    hd = D // heads
    outs = []
    for h in range(heads):
        cols = slice(h * hd, (h + 1) * hd)
        sc = _dot_nt(q[:, cols], mk_ref[:, cols])
        sc = sc - jnp.max(sc, axis=-1, keepdims=True)
        e = jnp.exp(sc)
        p = e / jnp.sum(e, axis=-1, keepdims=True)
        outs.append(_dot(p.astype(_BF), mv_ref[:, cols]).astype(_BF))
    o = jnp.concatenate(outs, axis=1)
    o_ref[...] = x1 + _dot(o, wo_ref[...])


def _mix_mem(x2d, ya, yb, wo1, wo2, nw, wq, mk, mv, wo, *, B, S, tm):
    T, D = x2d.shape
    M = mk.shape[0] // B
    nt = S // tm
    row = pl.BlockSpec((tm, D), lambda b, i: (b * nt + i, 0))
    memspec = pl.BlockSpec((M, D), lambda b, i: (b, 0))
    hd = D // MEM_HEADS
    return pl.pallas_call(
        functools.partial(_mixmem_kernel, heads=MEM_HEADS, q_scale=1.0 / math.sqrt(hd)),
        grid=(B, nt),
        in_specs=[row, row, row, _const_spec(wo1.shape), _const_spec(wo2.shape), _const_spec(nw.shape),
                  _const_spec(wq.shape), memspec, memspec, _const_spec(wo.shape)],
        out_specs=row,
        out_shape=jax.ShapeDtypeStruct((T, D), _F32),
        compiler_params=_params("arbitrary", "arbitrary"),
        name="mix_mem",
    )(x2d, ya, yb, wo1, wo2, nw, wq, mk, mv, wo)


def _ffn_kernel(x_ref, halo_ref, nw_ref, wup_ref, cw_ref, cb_ref, wdn_ref, fw_ref,
                o_ref, acc_ref, ug_ref, uv_ref, *, tm, nchunk, final_norm):
    i = pl.program_id(1)
    x = x_ref[...]
    hb = _rms(x, nw_ref[...]).astype(_BF)
    keep = (i > 0).astype(_F32)
    hh = (_rms(halo_ref[...], nw_ref[...]) * keep).astype(_BF)
    acc_ref[...] = jnp.zeros_like(acc_ref)

    def conv(u_ref, w_main, w_halo, cw, cb):
        u_ref[0:SUBLANES, :] = _dot(hh, w_halo)
        u_ref[SUBLANES:SUBLANES + tm, :] = _dot(hb, w_main)
        out = cb
        for kk in range(FFN_CONV):
            off = SUBLANES - (FFN_CONV - 1) + kk
            out = out + cw[kk:kk + 1, :] * u_ref[off:off + tm, :]
        return out

    def chunk(c, carry):
        wg = wup_ref[c]
        wv = wup_ref[nchunk + c]
        g = conv(ug_ref, wg, wg, cw_ref[c], cb_ref[c])
        val = conv(uv_ref, wv, wv, cw_ref[nchunk + c], cb_ref[nchunk + c])
        acc_ref[...] += _dot((_silu(g) * val).astype(_BF), wdn_ref[c])
        return carry

    lax.fori_loop(0, nchunk, chunk, 0)
    y = x + acc_ref[...]
    o_ref[...] = _rms(y, fw_ref[...]) if final_norm else y


def _ffn(x2d, nw, wup, cw, cb, wdn, fw, *, B, S, tm, final_norm):
    T, D = x2d.shape
    nchunk, fc, _ = wdn.shape
    nt = S // tm
    hb = tm // SUBLANES
    row = pl.BlockSpec((tm, D), lambda b, i: (b * nt + i, 0))
    halo = pl.BlockSpec((SUBLANES, D), lambda b, i: (jnp.maximum((b * nt + i) * hb - 1, 0), 0))
    return pl.pallas_call(
        functools.partial(_ffn_kernel, tm=tm, nchunk=nchunk, final_norm=final_norm),
        grid=(B, nt),
        in_specs=[row, halo] + [_const_spec(a.shape) for a in (nw, wup, cw, cb, wdn, fw)],
        out_specs=row,
        out_shape=jax.ShapeDtypeStruct((T, D), _F32),
        scratch_shapes=[pltpu.VMEM((tm, D), _F32),
                        pltpu.VMEM((tm + SUBLANES, fc), _F32),
                        pltpu.VMEM((tm + SUBLANES, fc), _F32)],
        compiler_params=_params("arbitrary", "arbitrary"),
        name="ffn",
    )(x2d, x2d, nw, wup, cw, cb, wdn, fw)


def _pad_lanes(a):
    return jnp.pad(a, ((0, 0), (0, LANES - a.shape[1])))


def kernel(x, mem, norm_mix_w, w_in, conv_ssd_w, conv_ssd_b, dt_bias, a_log, d_skip, ssd_norm_w, sb_norm_w, w_out, norm_mem_w, norm_memkv_w, w_mq, w_mk, w_mv, w_mo, norm_ffn_w, w_up, conv_ffn_w, conv_ffn_b, w_down, norm_final_w):
    B, S, D = x.shape
    T = B * S
    depth = w_in.shape[0]
    H = SSD_HEADS
    DI = ssd_norm_w.shape[1]
    C = conv_ssd_w.shape[2]
    W = sb_norm_w.shape[1]
    DFF = w_down.shape[1]
    o1, o2 = DI, DI + C
    o3 = o2 + H
    o4, o5 = o3 + W, o3 + 2 * W
    fc = FFN_CHUNK
    nchunk = DFF // fc
    assert DFF % fc == 0
    tm = min(512, S)

    x2d = x.reshape(T, D)
    mem2d = mem.reshape(B * mem.shape[1], D)
    for l in range(depth):
        wi = w_in[l]
        bf = lambda a: a.astype(_BF)
        z, xbc, dt_raw, q, k, v = _in_proj(
            x2d, norm_mix_w[l][None], bf(wi[:, :o1]), bf(wi[:, o1:o2]), bf(_pad_lanes(wi[:, o2:o3])),
            bf(wi[:, o3:o4]), bf(wi[:, o4:o5]), bf(wi[:, o5:]), tm=min(256, S))
        y_ssd = _ssd(z, xbc, dt_raw, conv_ssd_w[l], conv_ssd_b[l][None], _pad_lanes(dt_bias[l][None]),
                     _pad_lanes(a_log[l][None]), jnp.repeat(d_skip[l], DI // H)[None], ssd_norm_w[l][None],
                     B=B, S=S)
        y_sb = _sb_attention(q, k, v, sb_norm_w[l][None], B=B, S=S)
        mk, mv = _mem_kv(mem2d, norm_memkv_w[l][None], bf(w_mk[l]), bf(w_mv[l]))
        x2d = _mix_mem(x2d, y_ssd, y_sb, bf(w_out[l][:DI]), bf(w_out[l][DI:]), norm_mem_w[l][None],
                       bf(w_mq[l]), mk, mv, bf(w_mo[l]), B=B, S=S, tm=tm)
        wup = bf(w_up[l]).reshape(D, 2 * nchunk, fc).transpose(1, 0, 2)
        cw = conv_ffn_w[l].reshape(FFN_CONV, 2 * nchunk, fc).transpose(1, 0, 2)
        cb = conv_ffn_b[l].reshape(2 * nchunk, 1, fc)
        x2d = _ffn(x2d, norm_ffn_w[l][None], wup, cw, cb, bf(w_down[l]).reshape(nchunk, fc, D),
                   norm_final_w[None], B=B, S=S, tm=tm, final_norm=(l == depth - 1))
    return x2d.reshape(B, S, D)
```

```python
import functools
import math

import jax
import jax.numpy as jnp
from jax import lax
from jax.experimental import pallas as pl
from jax.experimental.pallas import tpu as pltpu

EPS = 1e-6
LANES = 128
SUBLANES = 8
VMEM_LIMIT = 56 * 1024 * 1024

SSD_HEADS = 16
SSD_GROUPS = 2
SSD_STATE = 128
SSD_CONV = 4
SSD_CHUNK = 128
SB_HEAD_DIM = 64
SB_TILE = 128
SB_PAIRS_PER_STEP = 4
MEM_HEADS = 4
FFN_CONV = 3
FFN_CHUNK = 256
FFN_HALO = 16

SB_SKIP_LOG = -110.0

_BF = jnp.bfloat16
_F32 = jnp.float32


def _dot(a, b):
    return jnp.dot(a, b, preferred_element_type=_F32)


def _dot_nt(a, b):
    return lax.dot_general(a, b, (((1,), (1,)), ((), ())), preferred_element_type=_F32)


def _rms(x, w):
    return x * lax.rsqrt(jnp.mean(x * x, axis=-1, keepdims=True) + EPS) * w


def _softplus(x):
    return jnp.maximum(x, 0.0) + jnp.log(1.0 + jnp.exp(-jnp.abs(x)))


def _silu(x):
    return x * jax.nn.sigmoid(x)


def _params(*sem):
    return pltpu.CompilerParams(dimension_semantics=sem, vmem_limit_bytes=VMEM_LIMIT)


def _const_spec(shape):
    nd = len(shape)
    return pl.BlockSpec(shape, lambda *_: (0,) * nd)


def _inproj_kernel(x_ref, nw_ref, wz_ref, wxbc_ref, wdt_ref, wq_ref, wk_ref, wv_ref,
                   z_ref, xbc_ref, dt_ref, q_ref, k_ref, v_ref, *, q_scale):
    hb = _rms(x_ref[...], nw_ref[...]).astype(_BF)
    z_ref[...] = _dot(hb, wz_ref[...])
    xbc_ref[...] = _dot(hb, wxbc_ref[...])
    dt_ref[...] = _dot(hb, wdt_ref[...])
    q_ref[...] = (_dot(hb, wq_ref[...]) * q_scale).astype(_BF)
    k_ref[...] = _dot(hb, wk_ref[...]).astype(_BF)
    v_ref[...] = _dot(hb, wv_ref[...]).astype(_BF)


def _in_proj(x2d, nw, wz, wxbc, wdt, wq, wk, wv, *, tm):
    T, D = x2d.shape
    row = lambda n: pl.BlockSpec((tm, n), lambda i: (i, 0))
    ws = (wz, wxbc, wdt, wq, wk, wv)
    out_dtypes = (_F32, _F32, _F32, _BF, _BF, _BF)
    return pl.pallas_call(
        functools.partial(_inproj_kernel, q_scale=1.0 / math.sqrt(SB_HEAD_DIM)),
        grid=(T // tm,),
        in_specs=[row(D), _const_spec(nw.shape)] + [_const_spec(w.shape) for w in ws],
        out_specs=[row(w.shape[1]) for w in ws],
        out_shape=[jax.ShapeDtypeStruct((T, w.shape[1]), dt) for w, dt in zip(ws, out_dtypes)],
        compiler_params=_params("arbitrary"),
        name="in_proj",
    )(x2d, nw, *ws)


def _split3(v):
    v1 = v.astype(_BF)
    r1 = v - v1.astype(_F32)
    v2 = r1.astype(_BF)
    v3 = (r1 - v2.astype(_F32)).astype(_BF)
    return v1, v2, v3


def _ssd_kernel(z_ref, xbc_ref, dt_ref, cw_ref, cb_ref, dtb_ref, alog_ref, dsk_ref, nw_ref,
                y_ref, tail_ref, state_ref, *, L, H, P, G, N):
    DI = H * P
    HG = H // G
    GW = DI // G
    c = pl.program_id(1)

    @pl.when(c == 0)
    def _():
        tail_ref[...] = jnp.zeros_like(tail_ref)
        state_ref[...] = jnp.zeros_like(state_ref)

    xbc = xbc_ref[...]
    ext = jnp.concatenate([tail_ref[...], xbc], axis=0)
    tail_ref[...] = xbc[L - SUBLANES:, :]
    conv = cb_ref[...] + cw_ref[SSD_CONV - 1:SSD_CONV, :] * xbc
    for kk in range(1, SSD_CONV):
        conv = conv + cw_ref[SSD_CONV - 1 - kk:SSD_CONV - kk, :] * pltpu.roll(ext, kk, 0)[SUBLANES:]
    act = _silu(conv)
    xs = act[:, :DI]
    Bm = act[:, DI:DI + G * N]
    Cm = act[:, DI + G * N:]

    dt = _softplus(dt_ref[...] + dtb_ref[...])
    a = dt * (-jnp.exp(alog_ref[...]))
    ri = lax.broadcasted_iota(jnp.int32, (L, L), 0)
    ci = lax.broadcasted_iota(jnp.int32, (L, L), 1)
    tril = ri >= ci
    tril_bf = jnp.where(tril, 1.0, 0.0).astype(_BF)
    a1, a2, a3 = _split3(a)
    a_cs = _dot(tril_bf, a1) + _dot(tril_bf, a2) + _dot(tril_bf, a3)
    a_last = a_cs[L - 1:L, :]
    a_cs_t = a_cs.T
    dt_t = dt.T
    w_col = jnp.exp(a_last - a_cs) * dt
    e_col = jnp.exp(a_cs)
    c_dec = jnp.exp(a_last)

    lane = lax.broadcasted_iota(jnp.int32, (1, 2 * P), 1)
    lo = lane < P

    for g in range(G):
        B_g = Bm[:, g * N:(g + 1) * N]
        C_g = Cm[:, g * N:(g + 1) * N].astype(_BF)
        cb = _dot_nt(C_g, B_g.astype(_BF))
        b_t = B_g.T.astype(_BF)
        ys = []
        for pr in range(HG // 2):
            h0 = g * HG + 2 * pr
            h1 = h0 + 1
            cols = slice(h0 * P, (h0 + 2) * P)
            xs2 = xs[:, cols]
            pick = lambda v: jnp.where(lo, v[:, h0:h0 + 1], v[:, h1:h1 + 1])
            prev2 = state_ref[:, cols]
            y_off = _dot(C_g, prev2.astype(_BF)) * pick(e_col)
            ms = []
            for hh in (h0, h1):
                seg = a_cs[:, hh:hh + 1] - a_cs_t[hh:hh + 1, :]
                lmat = jnp.where(tril, jnp.exp(jnp.where(tril, seg, 0.0)), 0.0)
                ms.append((cb * lmat * dt_t[hh:hh + 1, :]).astype(_BF))
            xs_lo = jnp.where(lo, xs2, 0.0).astype(_BF)
            xs_hi = jnp.where(lo, 0.0, xs2).astype(_BF)
            y_diag = _dot(jnp.concatenate(ms, axis=1), jnp.concatenate([xs_lo, xs_hi], axis=0))
            new_state = _dot(b_t, (xs2 * pick(w_col)).astype(_BF))
            state_ref[:, cols] = prev2 * pick(c_dec) + new_state
            y2 = y_diag + y_off + xs2 * dsk_ref[:, cols]
            ys.append(y2 * _silu(z_ref[:, cols]))
        yg = jnp.concatenate(ys, axis=1)
        y_ref[:, g * GW:(g + 1) * GW] = _rms(yg, nw_ref[:, g * GW:(g + 1) * GW]).astype(_BF)


def _ssd(z, xbc, dt_raw, cw, cb, dtb, alog, dsk, nw, *, B, S):
    T, DI = z.shape
    C = xbc.shape[1]
    L = SSD_CHUNK
    H, G, N = SSD_HEADS, SSD_GROUPS, SSD_STATE
    P = DI // H
    assert 2 * P == LANES and H % (2 * G) == 0 and S % L == 0
    nc = S // L
    row = lambda n: pl.BlockSpec((L, n), lambda b, c: (b * nc + c, 0))
    consts = (cw, cb, dtb, alog, dsk, nw)
    return pl.pallas_call(
        functools.partial(_ssd_kernel, L=L, H=H, P=P, G=G, N=N),
        grid=(B, nc),
        in_specs=[row(DI), row(C), row(LANES)] + [_const_spec(a.shape) for a in consts],
        out_specs=row(DI),
        out_shape=jax.ShapeDtypeStruct((T, DI), _BF),
        scratch_shapes=[pltpu.VMEM((SUBLANES, C), _F32), pltpu.VMEM((N, DI), _F32)],
        compiler_params=_params("arbitrary", "arbitrary"),
        name="ssd",
    )(z, xbc, dt_raw, *consts)


def _sb_kernel(q_ref, k_ref, v_ref, nw_ref, o_ref, acc_ref, r_ref, z_ref, p_ref, *, tq, hd, npp):
    i = pl.program_id(2)
    lane = lax.broadcasted_iota(jnp.int32, (1, 2 * hd), 1)
    lo = lane < hd
    nh = 2 * npp
    qcat = []
    for p in range(npp):
        q2 = q_ref[:, p * LANES:(p + 1) * LANES]
        zq = jnp.zeros_like(q2)
        qcat.append(jnp.concatenate([jnp.where(lo, q2, zq), jnp.where(lo, zq, q2)], axis=0))
    ri = lax.broadcasted_iota(jnp.int32, (tq, tq), 0)
    ci = lax.broadcasted_iota(jnp.int32, (tq, tq), 1)
    tri = jnp.where(ri >= ci, 1.0, 0.0).astype(_BF)
    tri2 = jnp.concatenate([tri, tri], axis=0)
    rs = lax.broadcasted_iota(jnp.int32, (nh * tq, tq), 0) & (tq - 1)
    cs = lax.broadcasted_iota(jnp.int32, (nh * tq, tq), 1)
    causal = cs < rs

    acc_ref[...] = jnp.zeros_like(acc_ref)
    r_ref[...] = jnp.zeros_like(r_ref)

    def key_start(j):
        return pl.multiple_of(jnp.maximum(j, 0) * tq, tq)

    def scores(j):
        start = key_start(j)
        return jnp.concatenate(
            [_dot_nt(qcat[p], k_ref[pl.ds(start, tq), p * LANES:(p + 1) * LANES]) for p in range(npp)],
            axis=0)

    def weights(diag):
        z = z_ref[...]
        lnb = -_softplus(z)
        if diag:
            lnb = jnp.where(causal, lnb, 0.0)
        r = r_ref[...]
        live = jnp.max(r[:, 0:1] + jnp.sum(lnb, axis=-1, keepdims=True)) > SB_SKIP_LOG
        hi = lnb.astype(_BF)
        lw = (lnb - hi.astype(_F32)).astype(_BF)
        cum = _dot(jnp.concatenate([hi, lw], axis=1), tri2)
        a = jnp.exp(z + cum + r)
        if diag:
            a = jnp.where(causal, a, 0.0)
        r_ref[...] = r + cum[:, 0:1]
        a = a.astype(_BF)
        for p in range(npp):
            p_ref[p] = jnp.concatenate(
                [a[2 * p * tq:(2 * p + 1) * tq], a[(2 * p + 1) * tq:(2 * p + 2) * tq]], axis=1)
        return live.astype(jnp.int32)

    def apply_weights(j):
        start = key_start(j)
        for p in range(npp):
            cols = slice(p * LANES, (p + 1) * LANES)
            v2 = v_ref[pl.ds(start, tq), cols]
            zv = jnp.zeros_like(v2)
            vv = jnp.concatenate([jnp.where(lo, v2, zv), jnp.where(lo, zv, v2)], axis=0)
            acc_ref[:, cols] += _dot(p_ref[p], vv)

    z_ref[...] = scores(i)
    z_next = scores(i - 1)
    live0 = weights(True)
    z_ref[...] = z_next

    def cond(carry):
        j, live = carry
        return jnp.logical_and(j >= 0, live > 0)

    def body(carry):
        j, _ = carry
        apply_weights(j + 1)
        z_next = scores(j - 1)
        live = weights(False)
        z_ref[...] = z_next
        return j - 1, live

    j_end, _ = lax.while_loop(cond, body, (i - 1, live0))
    apply_weights(j_end + 1)

    for p in range(npp):
        cols = slice(p * LANES, (p + 1) * LANES)
        acc = acc_ref[:, cols]
        sq = acc * acc
        s_lo = jnp.sum(jnp.where(lo, sq, 0.0), axis=-1, keepdims=True)
        s_hi = jnp.sum(jnp.where(lo, 0.0, sq), axis=-1, keepdims=True)
        inv = lax.rsqrt(jnp.where(lo, s_lo, s_hi) * (1.0 / hd) + EPS)
        o_ref[:, cols] = (acc * inv * nw_ref[:, cols]).astype(_BF)


def _sb_attention(q, k, v, nw, *, B, S):
    T, W = q.shape
    hd = SB_HEAD_DIM
    tq = SB_TILE
    npp = SB_PAIRS_PER_STEP
    cw = npp * LANES
    assert 2 * hd == LANES and tq == LANES and S % tq == 0 and W % cw == 0
    nq = S // tq
    k3 = k.reshape(B, S, W)
    v3 = v.reshape(B, S, W)
    return pl.pallas_call(
        functools.partial(_sb_kernel, tq=tq, hd=hd, npp=npp),
        grid=(B, W // cw, nq),
        in_specs=[
            pl.BlockSpec((tq, cw), lambda b, p, i: (b * nq + i, p)),
            pl.BlockSpec((None, S, cw), lambda b, p, i: (b, 0, p)),
            pl.BlockSpec((None, S, cw), lambda b, p, i: (b, 0, p)),
            pl.BlockSpec((1, cw), lambda b, p, i: (0, p)),
        ],
        out_specs=pl.BlockSpec((tq, cw), lambda b, p, i: (b * nq + i, p)),
        out_shape=jax.ShapeDtypeStruct((T, W), _BF),
        scratch_shapes=[pltpu.VMEM((tq, cw), _F32), pltpu.VMEM((2 * npp * tq, tq), _F32),
                        pltpu.VMEM((2 * npp * tq, tq), _F32), pltpu.VMEM((npp, tq, 2 * tq), _BF)],
        compiler_params=_params("arbitrary", "arbitrary", "arbitrary"),
        name="sb_attn",
    )(q, k3, v3, nw)


def _memkv_kernel(m_ref, nw_ref, wk_ref, wv_ref, k_ref, v_ref):
    mb = _rms(m_ref[...], nw_ref[...]).astype(_BF)
    k_ref[...] = _dot(mb, wk_ref[...]).astype(_BF)
    v_ref[...] = _dot(mb, wv_ref[...]).astype(_BF)


def _mem_kv(mem2d, nw, wk, wv):
    R, D = mem2d.shape
    return pl.pallas_call(
        _memkv_kernel,
        grid=(1,),
        in_specs=[_const_spec(a.shape) for a in (mem2d, nw, wk, wv)],
        out_specs=[_const_spec((R, D))] * 2,
        out_shape=[jax.ShapeDtypeStruct((R, D), _BF)] * 2,
        compiler_params=_params("arbitrary"),
        name="mem_kv",
    )(mem2d, nw, wk, wv)


def _mixmem_kernel(x_ref, ya_ref, yb_ref, wo1_ref, wo2_ref, nw_ref, wq_ref, mk_ref, mv_ref, wo_ref,
                   o_ref, *, heads, q_scale):
    x1 = x_ref[...] + _dot(ya_ref[...], wo1_ref[...]) + _dot(yb_ref[...], wo2_ref[...])
    hb = _rms(x1, nw_ref[...]).astype(_BF)
    q = (_dot(hb, wq_ref[...]) * q_scale).astype(_BF)
    D = q.shape[1]
    hd = D // heads
    outs = []
    for h in range(heads):
        cols = slice(h * hd, (h + 1) * hd)
        sc = _dot_nt(q[:, cols], mk_ref[:, cols])
        sc = sc - jnp.max(sc, axis=-1, keepdims=True)
        e = jnp.exp(sc)
        p = e / jnp.sum(e, axis=-1, keepdims=True)
        outs.append(_dot(p.astype(_BF), mv_ref[:, cols]).astype(_BF))
    o = jnp.concatenate(outs, axis=1)
    o_ref[...] = x1 + _dot(o, wo_ref[...])


def _mix_mem(x2d, ya, yb, wo1, wo2, nw, wq, mk, mv, wo, *, B, S, tm):
    T, D = x2d.shape
    M = mk.shape[0] // B
    nt = S // tm
    row = pl.BlockSpec((tm, D), lambda b, i: (b * nt + i, 0))
    memspec = pl.BlockSpec((M, D), lambda b, i: (b, 0))
    hd = D // MEM_HEADS
    return pl.pallas_call(
        functools.partial(_mixmem_kernel, heads=MEM_HEADS, q_scale=1.0 / math.sqrt(hd)),
        grid=(B, nt),
        in_specs=[row, row, row, _const_spec(wo1.shape), _const_spec(wo2.shape), _const_spec(nw.shape),
                  _const_spec(wq.shape), memspec, memspec, _const_spec(wo.shape)],
        out_specs=row,
        out_shape=jax.ShapeDtypeStruct((T, D), _F32),
        compiler_params=_params("arbitrary", "arbitrary"),
        name="mix_mem",
    )(x2d, ya, yb, wo1, wo2, nw, wq, mk, mv, wo)


def _ffn_kernel(x_ref, halo_ref, nw_ref, wup_ref, cw_ref, cb_ref, wdn_ref, fw_ref,
                o_ref, acc_ref, h_ref, ua_ref, ub_ref, aa_ref, ab_ref, *, tm, fc, nchunk, dff,
                final_norm):
    i = pl.program_id(1)
    x = x_ref[...]
    keep = (i > 0).astype(_F32)
    h_ref[...] = jnp.concatenate(
        [_rms(halo_ref[...], nw_ref[...]) * keep, _rms(x, nw_ref[...])], axis=0).astype(_BF)
    acc_ref[...] = jnp.zeros_like(acc_ref)

    ubuf = (ua_ref, ub_ref)
    abuf = (aa_ref, ab_ref)

    def cols(c, base):
        return slice(base + c * fc, base + (c + 1) * fc)

    def up(c):
        h = h_ref[...]
        ubuf[c % 2][0] = _dot(h, wup_ref[:, cols(c, 0)])
        ubuf[c % 2][1] = _dot(h, wup_ref[:, cols(c, dff)])

    def conv(u, cw, cb):
        out = cb + cw[FFN_CONV - 1:FFN_CONV, :] * u
        for kk in range(1, FFN_CONV):
            out = out + cw[FFN_CONV - 1 - kk:FFN_CONV - kk, :] * pltpu.roll(u, kk, 0)
        return out[FFN_HALO:]

    def gate(c):
        g = conv(ubuf[c % 2][0], cw_ref[:, cols(c, 0)], cb_ref[:, cols(c, 0)])
        val = conv(ubuf[c % 2][1], cw_ref[:, cols(c, dff)], cb_ref[:, cols(c, dff)])
        abuf[c % 2][...] = (_silu(g) * val).astype(_BF)

    def down(c):
        acc_ref[...] += _dot(abuf[c % 2][...], wdn_ref[cols(c, 0), :])

    up(0)
    if nchunk > 1:
        up(1)
    gate(0)
    for c in range(nchunk):
        down(c)
        if c + 1 < nchunk:
            gate(c + 1)
        if c + 2 < nchunk:
            up(c + 2)
    y = x + acc_ref[...]
    o_ref[...] = _rms(y, fw_ref[...]) if final_norm else y


def _ffn(x2d, nw, wup, cw, cb, wdn, fw, *, B, S, tm, final_norm):
    T, D = x2d.shape
    dff = wdn.shape[0]
    fc = FFN_CHUNK
    assert dff % fc == 0 and tm % FFN_HALO == 0
    nt = S // tm
    hb = tm // FFN_HALO
    row = pl.BlockSpec((tm, D), lambda b, i: (b * nt + i, 0))
    halo = pl.BlockSpec((FFN_HALO, D), lambda b, i: (jnp.maximum((b * nt + i) * hb - 1, 0), 0))
    return pl.pallas_call(
        functools.partial(_ffn_kernel, tm=tm, fc=fc, nchunk=dff // fc, dff=dff, final_norm=final_norm),
        grid=(B, nt),
        in_specs=[row, halo] + [_const_spec(a.shape) for a in (nw, wup, cw, cb, wdn, fw)],
        out_specs=row,
        out_shape=jax.ShapeDtypeStruct((T, D), _F32),
        scratch_shapes=[pltpu.VMEM((tm, D), _F32),
                        pltpu.VMEM((tm + FFN_HALO, D), _BF),
                        pltpu.VMEM((2, tm + FFN_HALO, fc), _F32),
                        pltpu.VMEM((2, tm + FFN_HALO, fc), _F32),
                        pltpu.VMEM((tm, fc), _BF),
                        pltpu.VMEM((tm, fc), _BF)],
        compiler_params=_params("arbitrary", "arbitrary"),
        name="ffn",
    )(x2d, x2d, nw, wup, cw, cb, wdn, fw)


def _pad_lanes(a):
    return jnp.pad(a, ((0, 0), (0, LANES - a.shape[1])))


def kernel(x, mem, norm_mix_w, w_in, conv_ssd_w, conv_ssd_b, dt_bias, a_log, d_skip, ssd_norm_w, sb_norm_w, w_out, norm_mem_w, norm_memkv_w, w_mq, w_mk, w_mv, w_mo, norm_ffn_w, w_up, conv_ffn_w, conv_ffn_b, w_down, norm_final_w):
    B, S, D = x.shape
    T = B * S
    depth = w_in.shape[0]
    H = SSD_HEADS
    DI = ssd_norm_w.shape[1]
    C = conv_ssd_w.shape[2]
    W = sb_norm_w.shape[1]
    o1, o2 = DI, DI + C
    o3 = o2 + H
    o4, o5 = o3 + W, o3 + 2 * W
    tm = min(512, S)

    x2d = x.reshape(T, D)
    mem2d = mem.reshape(B * mem.shape[1], D)
    for l in range(depth):
        wi = w_in[l]
        bf = lambda a: a.astype(_BF)
        z, xbc, dt_raw, q, k, v = _in_proj(
            x2d, norm_mix_w[l][None], bf(wi[:, :o1]), bf(wi[:, o1:o2]), bf(_pad_lanes(wi[:, o2:o3])),
            bf(wi[:, o3:o4]), bf(wi[:, o4:o5]), bf(wi[:, o5:]), tm=min(256, S))
        y_ssd = _ssd(z, xbc, dt_raw, conv_ssd_w[l], conv_ssd_b[l][None], _pad_lanes(dt_bias[l][None]),
                     _pad_lanes(a_log[l][None]), jnp.repeat(d_skip[l], DI // H)[None], ssd_norm_w[l][None],
                     B=B, S=S)
        y_sb = _sb_attention(q, k, v, sb_norm_w[l][None], B=B, S=S)
        mk, mv = _mem_kv(mem2d, norm_memkv_w[l][None], bf(w_mk[l]), bf(w_mv[l]))
        x2d = _mix_mem(x2d, y_ssd, y_sb, bf(w_out[l][:DI]), bf(w_out[l][DI:]), norm_mem_w[l][None],
                       bf(w_mq[l]), mk, mv, bf(w_mo[l]), B=B, S=S, tm=tm)
        x2d = _ffn(x2d, norm_ffn_w[l][None], bf(w_up[l]), conv_ffn_w[l], conv_ffn_b[l][None], bf(w_down[l]),
                   norm_final_w[None], B=B, S=S, tm=tm, final_norm=(l == depth - 1))
    return x2d.reshape(B, S, D)
```

```python
import functools
import math

import jax
import jax.numpy as jnp
from jax import lax
from jax.experimental import pallas as pl
from jax.experimental.pallas import tpu as pltpu

EPS = 1e-6
LANES = 128
SUBLANES = 8
VMEM_LIMIT = 56 * 1024 * 1024

SSD_HEADS = 16
SSD_GROUPS = 2
SSD_STATE = 128
SSD_CONV = 4
SSD_CHUNK = 128
SB_HEAD_DIM = 64
SB_TILE = 128
SB_PAIRS_PER_STEP = 4
SB_QTILES_PER_STEP = 8
MEM_HEADS = 4
FFN_CONV = 3
FFN_CHUNK = 256
FFN_HALO = 16

SB_SKIP_LOG = -110.0

_BF = jnp.bfloat16
_F32 = jnp.float32


def _dot(a, b):
    return jnp.dot(a, b, preferred_element_type=_F32)


def _dot_nt(a, b):
    return lax.dot_general(a, b, (((1,), (1,)), ((), ())), preferred_element_type=_F32)


def _rms(x, w):
    return x * lax.rsqrt(jnp.mean(x * x, axis=-1, keepdims=True) + EPS) * w


def _softplus(x):
    return jnp.maximum(x, 0.0) + jnp.log(1.0 + jnp.exp(-jnp.abs(x)))


def _silu(x):
    return x * jax.nn.sigmoid(x)


def _params(*sem):
    return pltpu.CompilerParams(dimension_semantics=sem, vmem_limit_bytes=VMEM_LIMIT)


def _const_spec(shape):
    nd = len(shape)
    return pl.BlockSpec(shape, lambda *_: (0,) * nd, pipeline_mode=pl.Buffered(1))


def _inproj_kernel(x_ref, nw_ref, wz_ref, wxbc_ref, wdt_ref, wq_ref, wk_ref, wv_ref,
                   z_ref, xbc_ref, dt_ref, q_ref, k_ref, v_ref, *, q_scale):
    hb = _rms(x_ref[...], nw_ref[...]).astype(_BF)
    z_ref[...] = _dot(hb, wz_ref[...])
    xbc_ref[...] = _dot(hb, wxbc_ref[...])
    dt_ref[...] = _dot(hb, wdt_ref[...])
    q_ref[...] = (_dot(hb, wq_ref[...]) * q_scale).astype(_BF)
    k_ref[...] = _dot(hb, wk_ref[...]).astype(_BF)
    v_ref[...] = _dot(hb, wv_ref[...]).astype(_BF)


def _in_proj(x2d, nw, wz, wxbc, wdt, wq, wk, wv, *, tm):
    T, D = x2d.shape
    row = lambda n: pl.BlockSpec((tm, n), lambda i: (i, 0))
    ws = (wz, wxbc, wdt, wq, wk, wv)
    out_dtypes = (_F32, _F32, _F32, _BF, _BF, _BF)
    return pl.pallas_call(
        functools.partial(_inproj_kernel, q_scale=1.0 / math.sqrt(SB_HEAD_DIM)),
        grid=(T // tm,),
        in_specs=[row(D), _const_spec(nw.shape)] + [_const_spec(w.shape) for w in ws],
        out_specs=[row(w.shape[1]) for w in ws],
        out_shape=[jax.ShapeDtypeStruct((T, w.shape[1]), dt) for w, dt in zip(ws, out_dtypes)],
        compiler_params=_params("arbitrary"),
        name="in_proj",
    )(x2d, nw, *ws)


def _split3(v):
    v1 = v.astype(_BF)
    r1 = v - v1.astype(_F32)
    v2 = r1.astype(_BF)
    v3 = (r1 - v2.astype(_F32)).astype(_BF)
    return v1, v2, v3


def _ssd_kernel(z_ref, xbc_ref, dt_ref, cw_ref, cb_ref, dtb_ref, alog_ref, dsk_ref, nw_ref,
                y_ref, tail_ref, state_ref, *, L, H, P, G, N):
    DI = H * P
    HG = H // G
    GW = DI // G
    c = pl.program_id(1)

    @pl.when(c == 0)
    def _():
        tail_ref[...] = jnp.zeros_like(tail_ref)
        state_ref[...] = jnp.zeros_like(state_ref)

    xbc = xbc_ref[...]
    ext = jnp.concatenate([tail_ref[...], xbc], axis=0)
    tail_ref[...] = xbc[L - SUBLANES:, :]
    conv = cb_ref[...] + cw_ref[SSD_CONV - 1:SSD_CONV, :] * xbc
    for kk in range(1, SSD_CONV):
        conv = conv + cw_ref[SSD_CONV - 1 - kk:SSD_CONV - kk, :] * pltpu.roll(ext, kk, 0)[SUBLANES:]
    act = _silu(conv)
    xs = act[:, :DI]
    Bm = act[:, DI:DI + G * N]
    Cm = act[:, DI + G * N:]

    dt = _softplus(dt_ref[...] + dtb_ref[...])
    a = dt * (-jnp.exp(alog_ref[...]))
    ri = lax.broadcasted_iota(jnp.int32, (L, L), 0)
    ci = lax.broadcasted_iota(jnp.int32, (L, L), 1)
    tril = ri >= ci
    tril_bf = jnp.where(tril, 1.0, 0.0).astype(_BF)
    a1, a2, a3 = _split3(a)
    a_cs = _dot(tril_bf, a1) + _dot(tril_bf, a2) + _dot(tril_bf, a3)
    a_last = a_cs[L - 1:L, :]
    a_cs_t = a_cs.T
    dt_t = dt.T
    w_col = jnp.exp(a_last - a_cs) * dt
    e_col = jnp.exp(a_cs)
    c_dec = jnp.exp(a_last)

    lane = lax.broadcasted_iota(jnp.int32, (1, 2 * P), 1)
    lo = lane < P

    for g in range(G):
        B_g = Bm[:, g * N:(g + 1) * N]
        C_g = Cm[:, g * N:(g + 1) * N].astype(_BF)
        cb = _dot_nt(C_g, B_g.astype(_BF))
        b_t = B_g.T.astype(_BF)
        ys = []
        for pr in range(HG // 2):
            h0 = g * HG + 2 * pr
            h1 = h0 + 1
            cols = slice(h0 * P, (h0 + 2) * P)
            xs2 = xs[:, cols]
            pick = lambda v: jnp.where(lo, v[:, h0:h0 + 1], v[:, h1:h1 + 1])
            prev2 = state_ref[:, cols]
            y_off = _dot(C_g, prev2.astype(_BF)) * pick(e_col)
            ms = []
            for hh in (h0, h1):
                seg = a_cs[:, hh:hh + 1] - a_cs_t[hh:hh + 1, :]
                lmat = jnp.exp(jnp.where(tril, seg, -jnp.inf))
                ms.append((cb * lmat * dt_t[hh:hh + 1, :]).astype(_BF))
            xs_lo = jnp.where(lo, xs2, 0.0).astype(_BF)
            xs_hi = jnp.where(lo, 0.0, xs2).astype(_BF)
            y_diag = _dot(jnp.concatenate(ms, axis=1), jnp.concatenate([xs_lo, xs_hi], axis=0))
            new_state = _dot(b_t, (xs2 * pick(w_col)).astype(_BF))
            state_ref[:, cols] = prev2 * pick(c_dec) + new_state
            y2 = y_diag + y_off + xs2 * dsk_ref[:, cols]
            ys.append(y2 * _silu(z_ref[:, cols]))
        yg = jnp.concatenate(ys, axis=1)
        y_ref[:, g * GW:(g + 1) * GW] = _rms(yg, nw_ref[:, g * GW:(g + 1) * GW]).astype(_BF)


def _ssd(z, xbc, dt_raw, cw, cb, dtb, alog, dsk, nw, *, B, S):
    T, DI = z.shape
    C = xbc.shape[1]
    L = SSD_CHUNK
    H, G, N = SSD_HEADS, SSD_GROUPS, SSD_STATE
    P = DI // H
    assert 2 * P == LANES and H % (2 * G) == 0 and S % L == 0
    nc = S // L
    row = lambda n: pl.BlockSpec((L, n), lambda b, c: (b * nc + c, 0))
    consts = (cw, cb, dtb, alog, dsk, nw)
    return pl.pallas_call(
        functools.partial(_ssd_kernel, L=L, H=H, P=P, G=G, N=N),
        grid=(B, nc),
        in_specs=[row(DI), row(C), row(LANES)] + [_const_spec(a.shape) for a in consts],
        out_specs=row(DI),
        out_shape=jax.ShapeDtypeStruct((T, DI), _BF),
        scratch_shapes=[pltpu.VMEM((SUBLANES, C), _F32), pltpu.VMEM((N, DI), _F32)],
        compiler_params=_params("arbitrary", "arbitrary"),
        name="ssd",
    )(z, xbc, dt_raw, *consts)


def _sb_kernel(q_ref, k_ref, v_ref, nw_ref, o_ref, acc_ref, r_ref, z_ref, p_ref, *, tq, hd, npp, nqb):
    lane = lax.broadcasted_iota(jnp.int32, (1, 2 * hd), 1)
    lo = lane < hd
    nh = 2 * npp
    ri = lax.broadcasted_iota(jnp.int32, (tq, tq), 0)
    ci = lax.broadcasted_iota(jnp.int32, (tq, tq), 1)
    tri = jnp.where(ri >= ci, 1.0, 0.0).astype(_BF)
    tri2 = jnp.concatenate([tri, tri], axis=0)
    rs = lax.broadcasted_iota(jnp.int32, (nh * tq, tq), 0) & (tq - 1)
    cs = lax.broadcasted_iota(jnp.int32, (nh * tq, tq), 1)
    causal = cs < rs
    first_tile = pl.program_id(2) * nqb

    def key_start(j):
        return pl.multiple_of(jnp.maximum(j, 0) * tq, tq)

    def weights(diag):
        z = z_ref[...]
        lnb = -_softplus(z)
        if diag:
            lnb = jnp.where(causal, lnb, 0.0)
        r = r_ref[...]
        live = jnp.max(r[:, 0:1] + jnp.sum(lnb, axis=-1, keepdims=True)) > SB_SKIP_LOG
        hi = lnb.astype(_BF)
        lw = (lnb - hi.astype(_F32)).astype(_BF)
        cum = _dot(jnp.concatenate([hi, lw], axis=1), tri2)
        a = jnp.exp(z + cum + r)
        if diag:
            a = jnp.where(causal, a, 0.0)
        r_ref[...] = r + cum[:, 0:1]
        a = a.astype(_BF)
        for p in range(npp):
            p_ref[p] = jnp.concatenate(
                [a[2 * p * tq:(2 * p + 1) * tq], a[(2 * p + 1) * tq:(2 * p + 2) * tq]], axis=1)
        return live.astype(jnp.int32)

    def apply_weights(j):
        start = key_start(j)
        for p in range(npp):
            cols = slice(p * LANES, (p + 1) * LANES)
            v2 = v_ref[pl.ds(start, tq), cols]
            zv = jnp.zeros_like(v2)
            vv = jnp.concatenate([jnp.where(lo, v2, zv), jnp.where(lo, zv, v2)], axis=0)
            acc_ref[:, cols] += _dot(p_ref[p], vv)

    def finish(qb):
        rows = pl.ds(pl.multiple_of(qb * tq, tq), tq)
        for p in range(npp):
            cols = slice(p * LANES, (p + 1) * LANES)
            acc = acc_ref[:, cols]
            sq = acc * acc
            s_lo = jnp.sum(jnp.where(lo, sq, 0.0), axis=-1, keepdims=True)
            s_hi = jnp.sum(jnp.where(lo, 0.0, sq), axis=-1, keepdims=True)
            inv = lax.rsqrt(jnp.where(lo, s_lo, s_hi) * (1.0 / hd) + EPS)
            o_ref[rows, cols] = (acc * inv * nw_ref[:, cols]).astype(_BF)

    acc_ref[...] = jnp.zeros_like(acc_ref)
    p_ref[...] = jnp.zeros_like(p_ref)

    def query_tile(qb, pending):
        i = first_tile + qb
        rows = pl.ds(pl.multiple_of(qb * tq, tq), tq)
        qcat = []
        for p in range(npp):
            q2 = q_ref[rows, p * LANES:(p + 1) * LANES]
            zq = jnp.zeros_like(q2)
            qcat.append(jnp.concatenate([jnp.where(lo, q2, zq), jnp.where(lo, zq, q2)], axis=0))

        def scores(j):
            start = key_start(j)
            return jnp.concatenate(
                [_dot_nt(qcat[p], k_ref[pl.ds(start, tq), p * LANES:(p + 1) * LANES]) for p in range(npp)],
                axis=0)

        z_first = scores(i)
        z_next = scores(i - 1)
        apply_weights(pending)
        finish(jnp.maximum(qb - 1, 0))
        acc_ref[...] = jnp.zeros_like(acc_ref)
        r_ref[...] = jnp.zeros_like(r_ref)
        z_ref[...] = z_first
        live0 = weights(True)
        z_ref[...] = z_next

        def cond(carry):
            j, live = carry
            return jnp.logical_and(j >= 0, live > 0)

        def body(carry):
            j, _ = carry
            apply_weights(j + 1)
            z_next = scores(j - 1)
            live = weights(False)
            z_ref[...] = z_next
            return j - 1, live

        j_end, _ = lax.while_loop(cond, body, (i - 1, live0))
        return j_end + 1

    pending = lax.fori_loop(0, nqb, query_tile, jnp.int32(0))
    apply_weights(pending)
    finish(nqb - 1)


def _sb_attention(q, k, v, nw, *, B, S):
    T, W = q.shape
    hd = SB_HEAD_DIM
    tq = SB_TILE
    npp = SB_PAIRS_PER_STEP
    cw = npp * LANES
    assert 2 * hd == LANES and tq == LANES and S % tq == 0 and W % cw == 0
    nqb = math.gcd(SB_QTILES_PER_STEP, S // tq)
    nq = S // (tq * nqb)
    k3 = k.reshape(B, S, W)
    v3 = v.reshape(B, S, W)
    return pl.pallas_call(
        functools.partial(_sb_kernel, tq=tq, hd=hd, npp=npp, nqb=nqb),
        grid=(B, W // cw, nq),
        in_specs=[
            pl.BlockSpec((nqb * tq, cw), lambda b, p, i: (b * nq + i, p)),
            pl.BlockSpec((None, S, cw), lambda b, p, i: (b, 0, p)),
            pl.BlockSpec((None, S, cw), lambda b, p, i: (b, 0, p)),
            pl.BlockSpec((1, cw), lambda b, p, i: (0, p)),
        ],
        out_specs=pl.BlockSpec((nqb * tq, cw), lambda b, p, i: (b * nq + i, p)),
        out_shape=jax.ShapeDtypeStruct((T, W), _BF),
        scratch_shapes=[pltpu.VMEM((tq, cw), _F32), pltpu.VMEM((2 * npp * tq, tq), _F32),
                        pltpu.VMEM((2 * npp * tq, tq), _F32), pltpu.VMEM((npp, tq, 2 * tq), _BF)],
        compiler_params=_params("arbitrary", "arbitrary", "arbitrary"),
        name="sb_attn",
    )(q, k3, v3, nw)


def _memkv_kernel(m_ref, nw_ref, wk_ref, wv_ref, k_ref, v_ref):
    mb = _rms(m_ref[...], nw_ref[...]).astype(_BF)
    k_ref[...] = _dot(mb, wk_ref[...]).astype(_BF)
    v_ref[...] = _dot(mb, wv_ref[...]).astype(_BF)


def _mem_kv(mem2d, nw, wk, wv):
    R, D = mem2d.shape
    return pl.pallas_call(
        _memkv_kernel,
        grid=(1,),
        in_specs=[_const_spec(a.shape) for a in (mem2d, nw, wk, wv)],
        out_specs=[_const_spec((R, D))] * 2,
        out_shape=[jax.ShapeDtypeStruct((R, D), _BF)] * 2,
        compiler_params=_params("arbitrary"),
        name="mem_kv",
    )(mem2d, nw, wk, wv)


def _mixmem_kernel(x_ref, ya_ref, yb_ref, wo1_ref, wo2_ref, nw_ref, wq_ref, mk_ref, mv_ref, wo_ref,
                   o_ref, *, heads, q_scale):
    x1 = x_ref[...] + _dot(ya_ref[...], wo1_ref[...]) + _dot(yb_ref[...], wo2_ref[...])
    hb = _rms(x1, nw_ref[...]).astype(_BF)
    q = (_dot(hb, wq_ref[...]) * q_scale).astype(_BF)
    D = q.shape[1]
    hd = D // heads
    outs = []
    for h in range(heads):
        cols = slice(h * hd, (h + 1) * hd)
        sc = _dot_nt(q[:, cols], mk_ref[:, cols])
        sc = sc - jnp.max(sc, axis=-1, keepdims=True)
        e = jnp.exp(sc)
        p = e / jnp.sum(e, axis=-1, keepdims=True)
        outs.append(_dot(p.astype(_BF), mv_ref[:, cols]).astype(_BF))
    o = jnp.concatenate(outs, axis=1)
    o_ref[...] = x1 + _dot(o, wo_ref[...])


def _mix_mem(x2d, ya, yb, wo1, wo2, nw, wq, mk, mv, wo, *, B, S, tm):
    T, D = x2d.shape
    M = mk.shape[0] // B
    nt = S // tm
    row = pl.BlockSpec((tm, D), lambda b, i: (b * nt + i, 0))
    memspec = pl.BlockSpec((M, D), lambda b, i: (b, 0))
    hd = D // MEM_HEADS
    return pl.pallas_call(
        functools.partial(_mixmem_kernel, heads=MEM_HEADS, q_scale=1.0 / math.sqrt(hd)),
        grid=(B, nt),
        in_specs=[row, row, row, _const_spec(wo1.shape), _const_spec(wo2.shape), _const_spec(nw.shape),
                  _const_spec(wq.shape), memspec, memspec, _const_spec(wo.shape)],
        out_specs=row,
        out_shape=jax.ShapeDtypeStruct((T, D), _F32),
        compiler_params=_params("arbitrary", "arbitrary"),
        name="mix_mem",
    )(x2d, ya, yb, wo1, wo2, nw, wq, mk, mv, wo)


def _ffn_kernel(x_ref, halo_ref, nw_ref, wup_ref, cw_ref, cb_ref, wdn_ref, fw_ref,
                o_ref, acc_ref, h_ref, ua_ref, ub_ref, aa_ref, ab_ref, *, tm, fc, nchunk, dff,
                final_norm):
    i = pl.program_id(1)
    x = x_ref[...]
    keep = (i > 0).astype(_F32)
    h_ref[...] = jnp.concatenate(
        [_rms(halo_ref[...], nw_ref[...]) * keep, _rms(x, nw_ref[...])], axis=0).astype(_BF)
    acc_ref[...] = jnp.zeros_like(acc_ref)

    ubuf = (ua_ref, ub_ref)
    abuf = (aa_ref, ab_ref)

    def cols(c, base):
        return slice(base + c * fc, base + (c + 1) * fc)

    def up(c):
        ubuf[c % 2][0] = _dot(h_ref[...], wup_ref[:, cols(c, 0)])
        ubuf[c % 2][1] = _dot(h_ref[...], wup_ref[:, cols(c, dff)])

    def conv(u, cw, cb):
        out = cb + cw[FFN_CONV - 1:FFN_CONV, :] * u
        for kk in range(1, FFN_CONV):
            out = out + cw[FFN_CONV - 1 - kk:FFN_CONV - kk, :] * pltpu.roll(u, kk, 0)
        return out[FFN_HALO:]

    def gate(c):
        g = conv(ubuf[c % 2][0], cw_ref[:, cols(c, 0)], cb_ref[:, cols(c, 0)])
        val = conv(ubuf[c % 2][1], cw_ref[:, cols(c, dff)], cb_ref[:, cols(c, dff)])
        abuf[c % 2][...] = (_silu(g) * val).astype(_BF)

    def down(c):
        acc_ref[...] += _dot(abuf[c % 2][...], wdn_ref[cols(c, 0), :])

    up(0)
    if nchunk > 1:
        up(1)
    gate(0)
    for c in range(nchunk):
        down(c)
        if c + 1 < nchunk:
            gate(c + 1)
        if c + 2 < nchunk:
            up(c + 2)
    y = x + acc_ref[...]
    o_ref[...] = _rms(y, fw_ref[...]) if final_norm else y


def _ffn(x2d, nw, wup, cw, cb, wdn, fw, *, B, S, tm, final_norm):
    T, D = x2d.shape
    dff = wdn.shape[0]
    fc = FFN_CHUNK
    assert dff % fc == 0 and tm % FFN_HALO == 0
    nt = S // tm
    hb = tm // FFN_HALO
    row = pl.BlockSpec((tm, D), lambda b, i: (b * nt + i, 0))
    halo = pl.BlockSpec((FFN_HALO, D), lambda b, i: (jnp.maximum((b * nt + i) * hb - 1, 0), 0))
    return pl.pallas_call(
        functools.partial(_ffn_kernel, tm=tm, fc=fc, nchunk=dff // fc, dff=dff, final_norm=final_norm),
        grid=(B, nt),
        in_specs=[row, halo] + [_const_spec(a.shape) for a in (nw, wup, cw, cb, wdn, fw)],
        out_specs=row,
        out_shape=jax.ShapeDtypeStruct((T, D), _F32),
        scratch_shapes=[pltpu.VMEM((tm, D), _F32),
                        pltpu.VMEM((tm + FFN_HALO, D), _BF),
                        pltpu.VMEM((2, tm + FFN_HALO, fc), _F32),
                        pltpu.VMEM((2, tm + FFN_HALO, fc), _F32),
                        pltpu.VMEM((tm, fc), _BF),
                        pltpu.VMEM((tm, fc), _BF)],
        compiler_params=_params("arbitrary", "arbitrary"),
        name="ffn",
    )(x2d, x2d, nw, wup, cw, cb, wdn, fw)


def _pad_lanes(a):
    return jnp.pad(a, ((0, 0), (0, LANES - a.shape[1])))


def kernel(x, mem, norm_mix_w, w_in, conv_ssd_w, conv_ssd_b, dt_bias, a_log, d_skip, ssd_norm_w, sb_norm_w, w_out, norm_mem_w, norm_memkv_w, w_mq, w_mk, w_mv, w_mo, norm_ffn_w, w_up, conv_ffn_w, conv_ffn_b, w_down, norm_final_w):
    B, S, D = x.shape
    T = B * S
    depth = w_in.shape[0]
    H = SSD_HEADS
    DI = ssd_norm_w.shape[1]
    C = conv_ssd_w.shape[2]
    W = sb_norm_w.shape[1]
    o1, o2 = DI, DI + C
    o3 = o2 + H
    o4, o5 = o3 + W, o3 + 2 * W
    tm = min(512, S)

    x2d = x.reshape(T, D)
    mem2d = mem.reshape(B * mem.shape[1], D)
    for l in range(depth):
        wi = w_in[l]
        bf = lambda a: a.astype(_BF)
        z, xbc, dt_raw, q, k, v = _in_proj(
            x2d, norm_mix_w[l][None], bf(wi[:, :o1]), bf(wi[:, o1:o2]), bf(_pad_lanes(wi[:, o2:o3])),
            bf(wi[:, o3:o4]), bf(wi[:, o4:o5]), bf(wi[:, o5:]), tm=min(512, S))
        y_ssd = _ssd(z, xbc, dt_raw, conv_ssd_w[l], conv_ssd_b[l][None], _pad_lanes(dt_bias[l][None]),
                     _pad_lanes(a_log[l][None]), jnp.repeat(d_skip[l], DI // H)[None], ssd_norm_w[l][None],
                     B=B, S=S)
        y_sb = _sb_attention(q, k, v, sb_norm_w[l][None], B=B, S=S)
        mk, mv = _mem_kv(mem2d, norm_memkv_w[l][None], bf(w_mk[l]), bf(w_mv[l]))
        x2d = _mix_mem(x2d, y_ssd, y_sb, bf(w_out[l][:DI]), bf(w_out[l][DI:]), norm_mem_w[l][None],
                       bf(w_mq[l]), mk, mv, bf(w_mo[l]), B=B, S=S, tm=tm)
        x2d = _ffn(x2d, norm_ffn_w[l][None], bf(w_up[l]), conv_ffn_w[l], conv_ffn_b[l][None], bf(w_down[l]),
                   norm_final_w[None], B=B, S=S, tm=tm, final_norm=(l == depth - 1))
    return x2d.reshape(B, S, D)
```

```python
import functools
import math

import jax
import jax.numpy as jnp
from jax import lax
from jax.experimental import pallas as pl
from jax.experimental.pallas import tpu as pltpu

EPS = 1e-6
LANES = 128
SUBLANES = 8
VMEM_LIMIT = 56 * 1024 * 1024

SSD_HEADS = 16
SSD_GROUPS = 2
SSD_STATE = 128
SSD_CONV = 4
SSD_CHUNK = 128
SB_HEAD_DIM = 64
SB_TILE = 128
SB_PAIRS_PER_STEP = 8
SB_QTILES_PER_STEP = 8
MEM_HEADS = 4
FFN_CONV = 3
FFN_CHUNK = 768
FFN_HALO = 16

SB_SKIP_LOG = -110.0

_BF = jnp.bfloat16
_F32 = jnp.float32


def _dot(a, b):
    return jnp.dot(a, b, preferred_element_type=_F32)


def _dot_nt(a, b):
    return lax.dot_general(a, b, (((1,), (1,)), ((), ())), preferred_element_type=_F32)


def _rms(x, w):
    return x * lax.rsqrt(jnp.mean(x * x, axis=-1, keepdims=True) + EPS) * w


def _softplus(x):
    return jnp.maximum(x, 0.0) + jnp.log(1.0 + jnp.exp(-jnp.abs(x)))


def _silu(x):
    return x * jax.nn.sigmoid(x)


def _params(*sem):
    return pltpu.CompilerParams(dimension_semantics=sem, vmem_limit_bytes=VMEM_LIMIT)


def _const_spec(shape):
    nd = len(shape)
    return pl.BlockSpec(shape, lambda *_: (0,) * nd, pipeline_mode=pl.Buffered(1))


def _inproj_kernel(x_ref, nw_ref, wz_ref, wxbc_ref, wdt_ref, wq_ref, wk_ref, wv_ref,
                   z_ref, xbc_ref, dt_ref, q_ref, k_ref, v_ref, *, q_scale):
    hb = _rms(x_ref[...], nw_ref[...]).astype(_BF)
    z_ref[...] = _dot(hb, wz_ref[...])
    xbc_ref[...] = _dot(hb, wxbc_ref[...])
    dt_ref[...] = _dot(hb, wdt_ref[...])
    q_ref[...] = (_dot(hb, wq_ref[...]) * q_scale).astype(_BF)
    k_ref[...] = _dot(hb, wk_ref[...]).astype(_BF)
    v_ref[...] = _dot(hb, wv_ref[...]).astype(_BF)


def _in_proj(x2d, nw, wz, wxbc, wdt, wq, wk, wv, *, tm):
    T, D = x2d.shape
    row = lambda n: pl.BlockSpec((tm, n), lambda i: (i, 0))
    ws = (wz, wxbc, wdt, wq, wk, wv)
    out_dtypes = (_F32, _F32, _F32, _BF, _BF, _BF)
    return pl.pallas_call(
        functools.partial(_inproj_kernel, q_scale=1.0 / math.sqrt(SB_HEAD_DIM)),
        grid=(T // tm,),
        in_specs=[row(D), _const_spec(nw.shape)] + [_const_spec(w.shape) for w in ws],
        out_specs=[row(w.shape[1]) for w in ws],
        out_shape=[jax.ShapeDtypeStruct((T, w.shape[1]), dt) for w, dt in zip(ws, out_dtypes)],
        compiler_params=_params("arbitrary"),
        name="in_proj",
    )(x2d, nw, *ws)


def _split3(v):
    v1 = v.astype(_BF)
    r1 = v - v1.astype(_F32)
    v2 = r1.astype(_BF)
    v3 = (r1 - v2.astype(_F32)).astype(_BF)
    return v1, v2, v3


def _ssd_kernel(z_ref, xbc_ref, dt_ref, cw_ref, cb_ref, dtb_ref, alog_ref, dsk_ref, nw_ref,
                y_ref, tail_ref, state_ref, *, L, H, P, G, N):
    DI = H * P
    HG = H // G
    GW = DI // G
    c = pl.program_id(1)

    @pl.when(c == 0)
    def _():
        tail_ref[...] = jnp.zeros_like(tail_ref)
        state_ref[...] = jnp.zeros_like(state_ref)

    xbc = xbc_ref[...]
    ext = jnp.concatenate([tail_ref[...], xbc], axis=0)
    tail_ref[...] = xbc[L - SUBLANES:, :]
    conv = cb_ref[...] + cw_ref[SSD_CONV - 1:SSD_CONV, :] * xbc
    for kk in range(1, SSD_CONV):
        conv = conv + cw_ref[SSD_CONV - 1 - kk:SSD_CONV - kk, :] * pltpu.roll(ext, kk, 0)[SUBLANES:]
    act = _silu(conv)
    xs = act[:, :DI]
    Bm = act[:, DI:DI + G * N]
    Cm = act[:, DI + G * N:]

    dt = _softplus(dt_ref[...] + dtb_ref[...])
    a = dt * (-jnp.exp(alog_ref[...]))
    ri = lax.broadcasted_iota(jnp.int32, (L, L), 0)
    ci = lax.broadcasted_iota(jnp.int32, (L, L), 1)
    tril = ri >= ci
    tril_bf = jnp.where(tril, 1.0, 0.0).astype(_BF)
    a1, a2, a3 = _split3(a)
    a_cs = _dot(tril_bf, a1) + _dot(tril_bf, a2) + _dot(tril_bf, a3)
    a_last = a_cs[L - 1:L, :]
    a_cs_t = a_cs.T
    dt_t = dt.T
    w_col = jnp.exp(a_last - a_cs) * dt
    e_col = jnp.exp(a_cs)
    c_dec = jnp.exp(a_last)

    lane = lax.broadcasted_iota(jnp.int32, (1, 2 * P), 1)
    lo = lane < P

    for g in range(G):
        B_g = Bm[:, g * N:(g + 1) * N]
        C_g = Cm[:, g * N:(g + 1) * N].astype(_BF)
        cb = _dot_nt(C_g, B_g.astype(_BF))
        b_t = B_g.T.astype(_BF)
        ys = []
        for pr in range(HG // 2):
            h0 = g * HG + 2 * pr
            h1 = h0 + 1
            cols = slice(h0 * P, (h0 + 2) * P)
            xs2 = xs[:, cols]
            pick = lambda v: jnp.where(lo, v[:, h0:h0 + 1], v[:, h1:h1 + 1])
            prev2 = state_ref[:, cols]
            y_off = _dot(C_g, prev2.astype(_BF)) * pick(e_col)
            ms = []
            for hh in (h0, h1):
                seg = a_cs[:, hh:hh + 1] - a_cs_t[hh:hh + 1, :]
                lmat = jnp.exp(jnp.where(tril, seg, -jnp.inf))
                ms.append((cb * lmat * dt_t[hh:hh + 1, :]).astype(_BF))
            xs_lo = jnp.where(lo, xs2, 0.0).astype(_BF)
            xs_hi = jnp.where(lo, 0.0, xs2).astype(_BF)
            y_diag = _dot(jnp.concatenate(ms, axis=1), jnp.concatenate([xs_lo, xs_hi], axis=0))
            new_state = _dot(b_t, (xs2 * pick(w_col)).astype(_BF))
            state_ref[:, cols] = prev2 * pick(c_dec) + new_state
            y2 = y_diag + y_off + xs2 * dsk_ref[:, cols]
            ys.append(y2 * _silu(z_ref[:, cols]))
        yg = jnp.concatenate(ys, axis=1)
        y_ref[:, g * GW:(g + 1) * GW] = _rms(yg, nw_ref[:, g * GW:(g + 1) * GW]).astype(_BF)


def _ssd(z, xbc, dt_raw, cw, cb, dtb, alog, dsk, nw, *, B, S):
    T, DI = z.shape
    C = xbc.shape[1]
    L = SSD_CHUNK
    H, G, N = SSD_HEADS, SSD_GROUPS, SSD_STATE
    P = DI // H
    assert 2 * P == LANES and H % (2 * G) == 0 and S % L == 0
    nc = S // L
    row = lambda n: pl.BlockSpec((L, n), lambda b, c: (b * nc + c, 0))
    consts = (cw, cb, dtb, alog, dsk, nw)
    return pl.pallas_call(
        functools.partial(_ssd_kernel, L=L, H=H, P=P, G=G, N=N),
        grid=(B, nc),
        in_specs=[row(DI), row(C), row(LANES)] + [_const_spec(a.shape) for a in consts],
        out_specs=row(DI),
        out_shape=jax.ShapeDtypeStruct((T, DI), _BF),
        scratch_shapes=[pltpu.VMEM((SUBLANES, C), _F32), pltpu.VMEM((N, DI), _F32)],
        compiler_params=_params("arbitrary", "arbitrary"),
        name="ssd",
    )(z, xbc, dt_raw, *consts)


def _sb_kernel(q_ref, k_ref, v_ref, nw_ref, o_ref, acc_ref, r_ref, z_ref, p_ref, *, tq, hd, npp, nqb):
    lane = lax.broadcasted_iota(jnp.int32, (1, 2 * hd), 1)
    lo = lane < hd
    nh = 2 * npp
    ri = lax.broadcasted_iota(jnp.int32, (tq, tq), 0)
    ci = lax.broadcasted_iota(jnp.int32, (tq, tq), 1)
    tri = jnp.where(ri >= ci, 1.0, 0.0).astype(_BF)
    tri2 = jnp.concatenate([tri, tri], axis=0)
    rs = lax.broadcasted_iota(jnp.int32, (nh * tq, tq), 0) & (tq - 1)
    cs = lax.broadcasted_iota(jnp.int32, (nh * tq, tq), 1)
    causal = cs < rs
    first_tile = pl.program_id(2) * nqb

    def key_start(j):
        return pl.multiple_of(jnp.maximum(j, 0) * tq, tq)

    def weights(diag):
        z = z_ref[...]
        lnb = -_softplus(z)
        if diag:
            lnb = jnp.where(causal, lnb, 0.0)
        r = r_ref[...]
        live = jnp.max(r[:, 0:1] + jnp.sum(lnb, axis=-1, keepdims=True)) > SB_SKIP_LOG
        hi = lnb.astype(_BF)
        lw = (lnb - hi.astype(_F32)).astype(_BF)
        cum = _dot(jnp.concatenate([hi, lw], axis=1), tri2)
        a = jnp.exp(z + cum + r)
        if diag:
            a = jnp.where(causal, a, 0.0)
        r_ref[...] = r + cum[:, 0:1]
        a = a.astype(_BF)
        for p in range(npp):
            p_ref[p] = jnp.concatenate(
                [a[2 * p * tq:(2 * p + 1) * tq], a[(2 * p + 1) * tq:(2 * p + 2) * tq]], axis=1)
        return live.astype(jnp.int32)

    def apply_weights(j):
        start = key_start(j)
        for p in range(npp):
            cols = slice(p * LANES, (p + 1) * LANES)
            v2 = v_ref[pl.ds(start, tq), cols]
            zv = jnp.zeros_like(v2)
            vv = jnp.concatenate([jnp.where(lo, v2, zv), jnp.where(lo, zv, v2)], axis=0)
            acc_ref[:, cols] += _dot(p_ref[p], vv)

    def finish(qb):
        rows = pl.ds(pl.multiple_of(qb * tq, tq), tq)
        for p in range(npp):
            cols = slice(p * LANES, (p + 1) * LANES)
            acc = acc_ref[:, cols]
            sq = acc * acc
            s_lo = jnp.sum(jnp.where(lo, sq, 0.0), axis=-1, keepdims=True)
            s_hi = jnp.sum(jnp.where(lo, 0.0, sq), axis=-1, keepdims=True)
            inv = lax.rsqrt(jnp.where(lo, s_lo, s_hi) * (1.0 / hd) + EPS)
            o_ref[rows, cols] = (acc * inv * nw_ref[:, cols]).astype(_BF)

    acc_ref[...] = jnp.zeros_like(acc_ref)
    p_ref[...] = jnp.zeros_like(p_ref)

    def query_tile(qb, pending):
        i = first_tile + qb
        rows = pl.ds(pl.multiple_of(qb * tq, tq), tq)
        qcat = []
        for p in range(npp):
            q2 = q_ref[rows, p * LANES:(p + 1) * LANES]
            zq = jnp.zeros_like(q2)
            qcat.append(jnp.concatenate([jnp.where(lo, q2, zq), jnp.where(lo, zq, q2)], axis=0))

        def scores(j):
            start = key_start(j)
            return jnp.concatenate(
                [_dot_nt(qcat[p], k_ref[pl.ds(start, tq), p * LANES:(p + 1) * LANES]) for p in range(npp)],
                axis=0)

        z_first = scores(i)
        z_next = scores(i - 1)
        apply_weights(pending)
        finish(jnp.maximum(qb - 1, 0))
        acc_ref[...] = jnp.zeros_like(acc_ref)
        r_ref[...] = jnp.zeros_like(r_ref)
        z_ref[...] = z_first
        live0 = weights(True)
        z_ref[...] = z_next

        def cond(carry):
            j, live = carry
            return jnp.logical_and(j >= 0, live > 0)

        def body(carry):
            j, _ = carry
            apply_weights(j + 1)
            z_next = scores(j - 1)
            live = weights(False)
            z_ref[...] = z_next
            return j - 1, live

        j_end, _ = lax.while_loop(cond, body, (i - 1, live0))
        return j_end + 1

    pending = lax.fori_loop(0, nqb, query_tile, jnp.int32(0))
    apply_weights(pending)
    finish(nqb - 1)


def _sb_attention(q, k, v, nw, *, B, S):
    T, W = q.shape
    hd = SB_HEAD_DIM
    tq = SB_TILE
    npp = SB_PAIRS_PER_STEP
    cw = npp * LANES
    assert 2 * hd == LANES and tq == LANES and S % tq == 0 and W % cw == 0
    nqb = math.gcd(SB_QTILES_PER_STEP, S // tq)
    nq = S // (tq * nqb)
    k3 = k.reshape(B, S, W)
    v3 = v.reshape(B, S, W)
    return pl.pallas_call(
        functools.partial(_sb_kernel, tq=tq, hd=hd, npp=npp, nqb=nqb),
        grid=(B, W // cw, nq),
        in_specs=[
            pl.BlockSpec((nqb * tq, cw), lambda b, p, i: (b * nq + i, p)),
            pl.BlockSpec((None, S, cw), lambda b, p, i: (b, 0, p), pipeline_mode=pl.Buffered(1)),
            pl.BlockSpec((None, S, cw), lambda b, p, i: (b, 0, p), pipeline_mode=pl.Buffered(1)),
            pl.BlockSpec((1, cw), lambda b, p, i: (0, p)),
        ],
        out_specs=pl.BlockSpec((nqb * tq, cw), lambda b, p, i: (b * nq + i, p)),
        out_shape=jax.ShapeDtypeStruct((T, W), _BF),
        scratch_shapes=[pltpu.VMEM((tq, cw), _F32), pltpu.VMEM((2 * npp * tq, tq), _F32),
                        pltpu.VMEM((2 * npp * tq, tq), _F32), pltpu.VMEM((npp, tq, 2 * tq), _BF)],
        compiler_params=_params("arbitrary", "arbitrary", "arbitrary"),
        name="sb_attn",
    )(q, k3, v3, nw)


def _memkv_kernel(m_ref, nw_ref, wk_ref, wv_ref, k_ref, v_ref):
    mb = _rms(m_ref[...], nw_ref[...]).astype(_BF)
    k_ref[...] = _dot(mb, wk_ref[...]).astype(_BF)
    v_ref[...] = _dot(mb, wv_ref[...]).astype(_BF)


def _mem_kv(mem2d, nw, wk, wv):
    R, D = mem2d.shape
    return pl.pallas_call(
        _memkv_kernel,
        grid=(1,),
        in_specs=[_const_spec(a.shape) for a in (mem2d, nw, wk, wv)],
        out_specs=[_const_spec((R, D))] * 2,
        out_shape=[jax.ShapeDtypeStruct((R, D), _BF)] * 2,
        compiler_params=_params("arbitrary"),
        name="mem_kv",
    )(mem2d, nw, wk, wv)


def _mixmem_kernel(x_ref, ya_ref, yb_ref, wo1_ref, wo2_ref, nw_ref, wq_ref, mk_ref, mv_ref, wo_ref,
                   o_ref, *, heads, q_scale):
    x1 = x_ref[...] + _dot(ya_ref[...], wo1_ref[...]) + _dot(yb_ref[...], wo2_ref[...])
    hb = _rms(x1, nw_ref[...]).astype(_BF)
    q = (_dot(hb, wq_ref[...]) * q_scale).astype(_BF)
    D = q.shape[1]
    hd = D // heads
    outs = []
    for h in range(heads):
        cols = slice(h * hd, (h + 1) * hd)
        sc = _dot_nt(q[:, cols], mk_ref[:, cols])
        sc = sc - jnp.max(sc, axis=-1, keepdims=True)
        e = jnp.exp(sc)
        p = e / jnp.sum(e, axis=-1, keepdims=True)
        outs.append(_dot(p.astype(_BF), mv_ref[:, cols]).astype(_BF))
    o = jnp.concatenate(outs, axis=1)
    o_ref[...] = x1 + _dot(o, wo_ref[...])


def _mix_mem(x2d, ya, yb, wo1, wo2, nw, wq, mk, mv, wo, *, B, S, tm):
    T, D = x2d.shape
    M = mk.shape[0] // B
    nt = S // tm
    row = pl.BlockSpec((tm, D), lambda b, i: (b * nt + i, 0))
    memspec = pl.BlockSpec((M, D), lambda b, i: (b, 0))
    hd = D // MEM_HEADS
    return pl.pallas_call(
        functools.partial(_mixmem_kernel, heads=MEM_HEADS, q_scale=1.0 / math.sqrt(hd)),
        grid=(B, nt),
        in_specs=[row, row, row, _const_spec(wo1.shape), _const_spec(wo2.shape), _const_spec(nw.shape),
                  _const_spec(wq.shape), memspec, memspec, _const_spec(wo.shape)],
        out_specs=row,
        out_shape=jax.ShapeDtypeStruct((T, D), _F32),
        compiler_params=_params("arbitrary", "arbitrary"),
        name="mix_mem",
    )(x2d, ya, yb, wo1, wo2, nw, wq, mk, mv, wo)


def _ffn_kernel(x_ref, halo_ref, nw_ref, wup_ref, cw_ref, cb_ref, wdn_ref, fw_ref,
                o_ref, acc_ref, h_ref, ua_ref, ub_ref, aa_ref, ab_ref, *, tm, fc, dff, final_norm):
    i = pl.program_id(1)
    x = x_ref[...]
    keep = (i > 0).astype(_F32)
    h_ref[...] = jnp.concatenate(
        [_rms(halo_ref[...], nw_ref[...]) * keep, _rms(x, nw_ref[...])], axis=0).astype(_BF)
    acc_ref[...] = jnp.zeros_like(acc_ref)

    ubuf = (ua_ref, ub_ref)
    abuf = (aa_ref, ab_ref)
    starts = list(range(0, dff, fc))
    widths = [min(fc, dff - s0) for s0 in starts]
    nchunk = len(starts)

    def cols(c, base):
        return slice(base + starts[c], base + starts[c] + widths[c])

    def up(c):
        w = widths[c]
        ubuf[c % 2][0, :, :w] = _dot(h_ref[...], wup_ref[:, cols(c, 0)])
        ubuf[c % 2][1, :, :w] = _dot(h_ref[...], wup_ref[:, cols(c, dff)])

    def conv(u, cw, cb):
        out = cb + cw[FFN_CONV - 1:FFN_CONV, :] * u
        for kk in range(1, FFN_CONV):
            out = out + cw[FFN_CONV - 1 - kk:FFN_CONV - kk, :] * pltpu.roll(u, kk, 0)
        return out[FFN_HALO:]

    def gate(c):
        w = widths[c]
        g = conv(ubuf[c % 2][0, :, :w], cw_ref[:, cols(c, 0)], cb_ref[:, cols(c, 0)])
        val = conv(ubuf[c % 2][1, :, :w], cw_ref[:, cols(c, dff)], cb_ref[:, cols(c, dff)])
        abuf[c % 2][:, :w] = (_silu(g) * val).astype(_BF)

    def down(c):
        acc_ref[...] += _dot(abuf[c % 2][:, :widths[c]], wdn_ref[cols(c, 0), :])

    up(0)
    if nchunk > 1:
        up(1)
    gate(0)
    for c in range(nchunk):
        down(c)
        if c + 1 < nchunk:
            gate(c + 1)
        if c + 2 < nchunk:
            up(c + 2)
    y = x + acc_ref[...]
    o_ref[...] = _rms(y, fw_ref[...]) if final_norm else y


def _ffn(x2d, nw, wup, cw, cb, wdn, fw, *, B, S, tm, final_norm):
    T, D = x2d.shape
    dff = wdn.shape[0]
    fc = FFN_CHUNK
    assert dff % LANES == 0 and tm % FFN_HALO == 0
    nt = S // tm
    hb = tm // FFN_HALO
    row = pl.BlockSpec((tm, D), lambda b, i: (b * nt + i, 0))
    halo = pl.BlockSpec((FFN_HALO, D), lambda b, i: (jnp.maximum((b * nt + i) * hb - 1, 0), 0))
    return pl.pallas_call(
        functools.partial(_ffn_kernel, tm=tm, fc=fc, dff=dff, final_norm=final_norm),
        grid=(B, nt),
        in_specs=[row, halo] + [_const_spec(a.shape) for a in (nw, wup, cw, cb, wdn, fw)],
        out_specs=row,
        out_shape=jax.ShapeDtypeStruct((T, D), _F32),
        scratch_shapes=[pltpu.VMEM((tm, D), _F32),
                        pltpu.VMEM((tm + FFN_HALO, D), _BF),
                        pltpu.VMEM((2, tm + FFN_HALO, fc), _F32),
                        pltpu.VMEM((2, tm + FFN_HALO, fc), _F32),
                        pltpu.VMEM((tm, fc), _BF),
                        pltpu.VMEM((tm, fc), _BF)],
        compiler_params=_params("arbitrary", "arbitrary"),
        name="ffn",
    )(x2d, x2d, nw, wup, cw, cb, wdn, fw)


def _pad_lanes(a):
    return jnp.pad(a, ((0, 0), (0, LANES - a.shape[1])))


def kernel(x, mem, norm_mix_w, w_in, conv_ssd_w, conv_ssd_b, dt_bias, a_log, d_skip, ssd_norm_w, sb_norm_w, w_out, norm_mem_w, norm_memkv_w, w_mq, w_mk, w_mv, w_mo, norm_ffn_w, w_up, conv_ffn_w, conv_ffn_b, w_down, norm_final_w):
    B, S, D = x.shape
    T = B * S
    depth = w_in.shape[0]
    H = SSD_HEADS
    DI = ssd_norm_w.shape[1]
    C = conv_ssd_w.shape[2]
    W = sb_norm_w.shape[1]
    o1, o2 = DI, DI + C
    o3 = o2 + H
    o4, o5 = o3 + W, o3 + 2 * W
    tm = min(512, S)

    x2d = x.reshape(T, D)
    mem2d = mem.reshape(B * mem.shape[1], D)
    for l in range(depth):
        wi = w_in[l]
        bf = lambda a: a.astype(_BF)
        z, xbc, dt_raw, q, k, v = _in_proj(
            x2d, norm_mix_w[l][None], bf(wi[:, :o1]), bf(wi[:, o1:o2]), bf(_pad_lanes(wi[:, o2:o3])),
            bf(wi[:, o3:o4]), bf(wi[:, o4:o5]), bf(wi[:, o5:]), tm=min(512, S))
        y_ssd = _ssd(z, xbc, dt_raw, conv_ssd_w[l], conv_ssd_b[l][None], _pad_lanes(dt_bias[l][None]),
                     _pad_lanes(a_log[l][None]), jnp.repeat(d_skip[l], DI // H)[None], ssd_norm_w[l][None],
                     B=B, S=S)
        y_sb = _sb_attention(q, k, v, sb_norm_w[l][None], B=B, S=S)
        mk, mv = _mem_kv(mem2d, norm_memkv_w[l][None], bf(w_mk[l]), bf(w_mv[l]))
        x2d = _mix_mem(x2d, y_ssd, y_sb, bf(w_out[l][:DI]), bf(w_out[l][DI:]), norm_mem_w[l][None],
                       bf(w_mq[l]), mk, mv, bf(w_mo[l]), B=B, S=S, tm=tm)
        x2d = _ffn(x2d, norm_ffn_w[l][None], bf(w_up[l]), conv_ffn_w[l], conv_ffn_b[l][None], bf(w_down[l]),
                   norm_final_w[None], B=B, S=S, tm=tm, final_norm=(l == depth - 1))
    return x2d.reshape(B, S, D)
```

```python
import functools
import math

import jax
import jax.numpy as jnp
from jax import lax
from jax.experimental import pallas as pl
from jax.experimental.pallas import tpu as pltpu

EPS = 1e-6
LANES = 128
SUBLANES = 8
VMEM_LIMIT = 56 * 1024 * 1024

SSD_HEADS = 16
SSD_GROUPS = 2
SSD_STATE = 128
SSD_CONV = 4
SSD_CHUNK = 128
SB_HEAD_DIM = 64
SB_TILE = 128
SB_PAIRS_PER_STEP = 8
SB_TOP_ROWS = 64
SB_QTILES_PER_STEP = 8
MEM_HEADS = 4
FFN_CONV = 3
FFN_CHUNK = 768
FFN_HALO = 16

SB_SKIP_LOG = -110.0

_BF = jnp.bfloat16
_F32 = jnp.float32


def _dot(a, b):
    return jnp.dot(a, b, preferred_element_type=_F32)


def _dot_nt(a, b):
    return lax.dot_general(a, b, (((1,), (1,)), ((), ())), preferred_element_type=_F32)


def _rms(x, w):
    return x * lax.rsqrt(jnp.mean(x * x, axis=-1, keepdims=True) + EPS) * w


def _softplus(x):
    return jnp.maximum(x, 0.0) + jnp.log(1.0 + jnp.exp(-jnp.abs(x)))


def _silu(x):
    return x * jax.nn.sigmoid(x)


def _params(*sem):
    return pltpu.CompilerParams(dimension_semantics=sem, vmem_limit_bytes=VMEM_LIMIT)


def _const_spec(shape):
    nd = len(shape)
    return pl.BlockSpec(shape, lambda *_: (0,) * nd, pipeline_mode=pl.Buffered(1))


def _inproj_kernel(x_ref, nw_ref, wz_ref, wxbc_ref, wdt_ref, wq_ref, wk_ref, wv_ref,
                   z_ref, xbc_ref, dt_ref, q_ref, k_ref, v_ref, *, q_scale):
    hb = _rms(x_ref[...], nw_ref[...]).astype(_BF)
    z_ref[...] = _dot(hb, wz_ref[...])
    xbc_ref[...] = _dot(hb, wxbc_ref[...])
    dt_ref[...] = _dot(hb, wdt_ref[...])
    q_ref[...] = (_dot(hb, wq_ref[...]) * q_scale).astype(_BF)
    k_ref[...] = _dot(hb, wk_ref[...]).astype(_BF)
    v_ref[...] = _dot(hb, wv_ref[...]).astype(_BF)


def _in_proj(x2d, nw, wz, wxbc, wdt, wq, wk, wv, *, tm):
    T, D = x2d.shape
    row = lambda n: pl.BlockSpec((tm, n), lambda i: (i, 0))
    ws = (wz, wxbc, wdt, wq, wk, wv)
    out_dtypes = (_F32, _F32, _F32, _BF, _BF, _BF)
    return pl.pallas_call(
        functools.partial(_inproj_kernel, q_scale=1.0 / math.sqrt(SB_HEAD_DIM)),
        grid=(T // tm,),
        in_specs=[row(D), _const_spec(nw.shape)] + [_const_spec(w.shape) for w in ws],
        out_specs=[row(w.shape[1]) for w in ws],
        out_shape=[jax.ShapeDtypeStruct((T, w.shape[1]), dt) for w, dt in zip(ws, out_dtypes)],
        compiler_params=_params("arbitrary"),
        name="in_proj",
    )(x2d, nw, *ws)


def _split3(v):
    v1 = v.astype(_BF)
    r1 = v - v1.astype(_F32)
    v2 = r1.astype(_BF)
    v3 = (r1 - v2.astype(_F32)).astype(_BF)
    return v1, v2, v3


def _ssd_kernel(z_ref, xbc_ref, dt_ref, cw_ref, cb_ref, dtb_ref, alog_ref, dsk_ref, nw_ref,
                y_ref, tail_ref, state_ref, *, L, H, P, G, N):
    DI = H * P
    HG = H // G
    GW = DI // G
    c = pl.program_id(1)

    @pl.when(c == 0)
    def _():
        tail_ref[...] = jnp.zeros_like(tail_ref)
        state_ref[...] = jnp.zeros_like(state_ref)

    xbc = xbc_ref[...]
    ext = jnp.concatenate([tail_ref[...], xbc], axis=0)
    tail_ref[...] = xbc[L - SUBLANES:, :]
    conv = cb_ref[...] + cw_ref[SSD_CONV - 1:SSD_CONV, :] * xbc
    for kk in range(1, SSD_CONV):
        conv = conv + cw_ref[SSD_CONV - 1 - kk:SSD_CONV - kk, :] * pltpu.roll(ext, kk, 0)[SUBLANES:]
    act = _silu(conv)
    xs = act[:, :DI]
    Bm = act[:, DI:DI + G * N]
    Cm = act[:, DI + G * N:]

    dt = _softplus(dt_ref[...] + dtb_ref[...])
    a = dt * (-jnp.exp(alog_ref[...]))
    ri = lax.broadcasted_iota(jnp.int32, (L, L), 0)
    ci = lax.broadcasted_iota(jnp.int32, (L, L), 1)
    tril = ri >= ci
    tril_bf = jnp.where(tril, 1.0, 0.0).astype(_BF)
    a1, a2, a3 = _split3(a)
    a_cs = _dot(tril_bf, a1) + _dot(tril_bf, a2) + _dot(tril_bf, a3)
    a_last = a_cs[L - 1:L, :]
    a_cs_t = a_cs.T
    dt_t = dt.T
    w_col = jnp.exp(a_last - a_cs) * dt
    e_col = jnp.exp(a_cs)
    c_dec = jnp.exp(a_last)

    lane = lax.broadcasted_iota(jnp.int32, (1, 2 * P), 1)
    lo = lane < P

    for g in range(G):
        B_g = Bm[:, g * N:(g + 1) * N]
        C_g = Cm[:, g * N:(g + 1) * N].astype(_BF)
        cb = _dot_nt(C_g, B_g.astype(_BF))
        b_t = B_g.T.astype(_BF)
        ys = []
        for pr in range(HG // 2):
            h0 = g * HG + 2 * pr
            h1 = h0 + 1
            cols = slice(h0 * P, (h0 + 2) * P)
            xs2 = xs[:, cols]
            pick = lambda v: jnp.where(lo, v[:, h0:h0 + 1], v[:, h1:h1 + 1])
            prev2 = state_ref[:, cols]
            y_off = _dot(C_g, prev2.astype(_BF)) * pick(e_col)
            ms = []
            for hh in (h0, h1):
                seg = a_cs[:, hh:hh + 1] - a_cs_t[hh:hh + 1, :]
                lmat = jnp.exp(jnp.where(tril, seg, -jnp.inf))
                ms.append((cb * lmat * dt_t[hh:hh + 1, :]).astype(_BF))
            xs_lo = jnp.where(lo, xs2, 0.0).astype(_BF)
            xs_hi = jnp.where(lo, 0.0, xs2).astype(_BF)
            y_diag = _dot(jnp.concatenate(ms, axis=1), jnp.concatenate([xs_lo, xs_hi], axis=0))
            new_state = _dot(b_t, (xs2 * pick(w_col)).astype(_BF))
            state_ref[:, cols] = prev2 * pick(c_dec) + new_state
            y2 = y_diag + y_off + xs2 * dsk_ref[:, cols]
            ys.append(y2 * _silu(z_ref[:, cols]))
        yg = jnp.concatenate(ys, axis=1)
        y_ref[:, g * GW:(g + 1) * GW] = _rms(yg, nw_ref[:, g * GW:(g + 1) * GW]).astype(_BF)


def _ssd(z, xbc, dt_raw, cw, cb, dtb, alog, dsk, nw, *, B, S):
    T, DI = z.shape
    C = xbc.shape[1]
    L = SSD_CHUNK
    H, G, N = SSD_HEADS, SSD_GROUPS, SSD_STATE
    P = DI // H
    assert 2 * P == LANES and H % (2 * G) == 0 and S % L == 0
    nc = S // L
    row = lambda n: pl.BlockSpec((L, n), lambda b, c: (b * nc + c, 0))
    consts = (cw, cb, dtb, alog, dsk, nw)
    return pl.pallas_call(
        functools.partial(_ssd_kernel, L=L, H=H, P=P, G=G, N=N),
        grid=(B, nc),
        in_specs=[row(DI), row(C), row(LANES)] + [_const_spec(a.shape) for a in consts],
        out_specs=row(DI),
        out_shape=jax.ShapeDtypeStruct((T, DI), _BF),
        scratch_shapes=[pltpu.VMEM((SUBLANES, C), _F32), pltpu.VMEM((N, DI), _F32)],
        compiler_params=_params("arbitrary", "arbitrary"),
        name="ssd",
    )(z, xbc, dt_raw, *consts)


def _sb_kernel(q_ref, k_ref, v_ref, nw_ref, o_ref, acc_ref, r_ref, z_ref, p_ref, *, tq, hd, npp, nqb):
    lane = lax.broadcasted_iota(jnp.int32, (1, 2 * hd), 1)
    lo = lane < hd
    nh = 2 * npp
    ri = lax.broadcasted_iota(jnp.int32, (tq, tq), 0)
    ci = lax.broadcasted_iota(jnp.int32, (tq, tq), 1)
    tri = jnp.where(ri >= ci, 1.0, 0.0).astype(_BF)
    tri2 = jnp.concatenate([tri, tri], axis=0)
    rs = lax.broadcasted_iota(jnp.int32, (nh * tq, tq), 0) & (tq - 1)
    cs = lax.broadcasted_iota(jnp.int32, (nh * tq, tq), 1)
    causal = cs < rs
    first_tile = pl.program_id(2) * nqb

    def key_start(j):
        return pl.multiple_of(jnp.maximum(j, 0) * tq, tq)

    def weights(mode):
        nr = SB_TOP_ROWS if mode == "top" else tq

        def head_rows(ref):
            if mode != "top":
                return ref[...]
            return jnp.concatenate([ref[h * tq:h * tq + nr, :] for h in range(nh)], axis=0)

        z = head_rows(z_ref)
        lnb = -_softplus(z)
        if mode == "diag":
            lnb = jnp.where(causal, lnb, 0.0)
        r = head_rows(r_ref)
        one = jnp.int32(1)
        if mode == "diag":
            live_top, live_rest = one, one
        else:
            tot = r[:, 0:1] + jnp.sum(lnb, axis=-1, keepdims=True)
            if mode == "top":
                live_top, live_rest = (jnp.max(tot) > SB_SKIP_LOG).astype(jnp.int32), jnp.int32(0)
            else:
                pick = lambda lo_row, hi_row: jnp.concatenate(
                    [tot[h * tq + lo_row:h * tq + hi_row] for h in range(nh)], axis=0)
                live_top = (jnp.max(pick(0, SB_TOP_ROWS)) > SB_SKIP_LOG).astype(jnp.int32)
                live_rest = (jnp.max(pick(SB_TOP_ROWS, tq)) > SB_SKIP_LOG).astype(jnp.int32)
        hi = lnb.astype(_BF)
        lw = (lnb - hi.astype(_F32)).astype(_BF)
        cum = _dot(jnp.concatenate([hi, lw], axis=1), tri2)
        a = jnp.exp(z + cum + r)
        if mode == "diag":
            a = jnp.where(causal, a, 0.0)
        r_new = r + cum[:, 0:1]
        if mode == "top":
            for h in range(nh):
                r_ref[h * tq:h * tq + nr, :] = r_new[h * nr:(h + 1) * nr]
        else:
            r_ref[...] = r_new
        a = a.astype(_BF)
        for p in range(npp):
            p_ref[p, 0:nr, :] = jnp.concatenate(
                [a[2 * p * nr:(2 * p + 1) * nr], a[(2 * p + 1) * nr:(2 * p + 2) * nr]], axis=1)
            if mode == "top":
                p_ref[p, nr:tq, :] = jnp.zeros((tq - nr, 2 * tq), _BF)
        return live_top, live_rest

    def apply_weights(j):
        start = key_start(j)
        for p in range(npp):
            cols = slice(p * LANES, (p + 1) * LANES)
            v2 = v_ref[pl.ds(start, tq), cols]
            zv = jnp.zeros_like(v2)
            vv = jnp.concatenate([jnp.where(lo, v2, zv), jnp.where(lo, zv, v2)], axis=0)
            acc_ref[:, cols] += _dot(p_ref[p], vv)

    def finish(qb):
        rows = pl.ds(pl.multiple_of(qb * tq, tq), tq)
        for p in range(npp):
            cols = slice(p * LANES, (p + 1) * LANES)
            acc = acc_ref[:, cols]
            sq = acc * acc
            s_lo = jnp.sum(jnp.where(lo, sq, 0.0), axis=-1, keepdims=True)
            s_hi = jnp.sum(jnp.where(lo, 0.0, sq), axis=-1, keepdims=True)
            inv = lax.rsqrt(jnp.where(lo, s_lo, s_hi) * (1.0 / hd) + EPS)
            o_ref[rows, cols] = (acc * inv * nw_ref[:, cols]).astype(_BF)

    acc_ref[...] = jnp.zeros_like(acc_ref)
    p_ref[...] = jnp.zeros_like(p_ref)

    def query_tile(qb, pending):
        i = first_tile + qb
        rows = pl.ds(pl.multiple_of(qb * tq, tq), tq)
        qcat = []
        for p in range(npp):
            q2 = q_ref[rows, p * LANES:(p + 1) * LANES]
            zq = jnp.zeros_like(q2)
            qcat.append(jnp.concatenate([jnp.where(lo, q2, zq), jnp.where(lo, zq, q2)], axis=0))

        def scores(j):
            start = key_start(j)
            return jnp.concatenate(
                [_dot_nt(qcat[p], k_ref[pl.ds(start, tq), p * LANES:(p + 1) * LANES]) for p in range(npp)],
                axis=0)

        z_first = scores(i)
        z_next = scores(i - 1)
        apply_weights(pending)
        finish(jnp.maximum(qb - 1, 0))
        acc_ref[...] = jnp.zeros_like(acc_ref)
        r_ref[...] = jnp.zeros_like(r_ref)
        z_ref[...] = z_first
        live_top0, live_rest0 = weights("diag")
        z_ref[...] = z_next

        def cond(carry):
            j, live_top, live_rest = carry
            return jnp.logical_and(j >= 0, (live_top + live_rest) > 0)

        def body(carry):
            j, _, live_rest = carry

            def step(mode):
                def run():
                    apply_weights(j + 1)
                    z_next = scores(j - 1)
                    live = weights(mode)
                    z_ref[...] = z_next
                    return live
                return run

            live_top, live_rest = lax.cond(live_rest > 0, step("full"), step("top"))
            return j - 1, live_top, live_rest

        j_end, _, _ = lax.while_loop(cond, body, (i - 1, live_top0, live_rest0))
        return j_end + 1

    pending = lax.fori_loop(0, nqb, query_tile, jnp.int32(0))
    apply_weights(pending)
    finish(nqb - 1)


def _sb_attention(q, k, v, nw, *, B, S):
    T, W = q.shape
    hd = SB_HEAD_DIM
    tq = SB_TILE
    npp = SB_PAIRS_PER_STEP
    cw = npp * LANES
    assert 2 * hd == LANES and tq == LANES and S % tq == 0 and W % cw == 0
    nqb = math.gcd(SB_QTILES_PER_STEP, S // tq)
    nq = S // (tq * nqb)
    k3 = k.reshape(B, S, W)
    v3 = v.reshape(B, S, W)
    return pl.pallas_call(
        functools.partial(_sb_kernel, tq=tq, hd=hd, npp=npp, nqb=nqb),
        grid=(B, W // cw, nq),
        in_specs=[
            pl.BlockSpec((nqb * tq, cw), lambda b, p, i: (b * nq + i, p)),
            pl.BlockSpec((None, S, cw), lambda b, p, i: (b, 0, p), pipeline_mode=pl.Buffered(1)),
            pl.BlockSpec((None, S, cw), lambda b, p, i: (b, 0, p), pipeline_mode=pl.Buffered(1)),
            pl.BlockSpec((1, cw), lambda b, p, i: (0, p)),
        ],
        out_specs=pl.BlockSpec((nqb * tq, cw), lambda b, p, i: (b * nq + i, p)),
        out_shape=jax.ShapeDtypeStruct((T, W), _BF),
        scratch_shapes=[pltpu.VMEM((tq, cw), _F32), pltpu.VMEM((2 * npp * tq, tq), _F32),
                        pltpu.VMEM((2 * npp * tq, tq), _F32), pltpu.VMEM((npp, tq, 2 * tq), _BF)],
        compiler_params=_params("arbitrary", "arbitrary", "arbitrary"),
        name="sb_attn",
    )(q, k3, v3, nw)


def _memkv_kernel(m_ref, nw_ref, wk_ref, wv_ref, k_ref, v_ref):
    mb = _rms(m_ref[...], nw_ref[...]).astype(_BF)
    k_ref[...] = _dot(mb, wk_ref[...]).astype(_BF)
    v_ref[...] = _dot(mb, wv_ref[...]).astype(_BF)


def _mem_kv(mem2d, nw, wk, wv):
    R, D = mem2d.shape
    return pl.pallas_call(
        _memkv_kernel,
        grid=(1,),
        in_specs=[_const_spec(a.shape) for a in (mem2d, nw, wk, wv)],
        out_specs=[_const_spec((R, D))] * 2,
        out_shape=[jax.ShapeDtypeStruct((R, D), _BF)] * 2,
        compiler_params=_params("arbitrary"),
        name="mem_kv",
    )(mem2d, nw, wk, wv)


def _mixmem_kernel(x_ref, ya_ref, yb_ref, wo1_ref, wo2_ref, nw_ref, wq_ref, mk_ref, mv_ref, wo_ref,
                   o_ref, *, heads, q_scale):
    x1 = x_ref[...] + _dot(ya_ref[...], wo1_ref[...]) + _dot(yb_ref[...], wo2_ref[...])
    hb = _rms(x1, nw_ref[...]).astype(_BF)
    q = (_dot(hb, wq_ref[...]) * q_scale).astype(_BF)
    D = q.shape[1]
    hd = D // heads
    outs = []
    for h in range(heads):
        cols = slice(h * hd, (h + 1) * hd)
        sc = _dot_nt(q[:, cols], mk_ref[:, cols])
        sc = sc - jnp.max(sc, axis=-1, keepdims=True)
        e = jnp.exp(sc)
        p = e / jnp.sum(e, axis=-1, keepdims=True)
        outs.append(_dot(p.astype(_BF), mv_ref[:, cols]).astype(_BF))
    o = jnp.concatenate(outs, axis=1)
    o_ref[...] = x1 + _dot(o, wo_ref[...])


def _mix_mem(x2d, ya, yb, wo1, wo2, nw, wq, mk, mv, wo, *, B, S, tm):
    T, D = x2d.shape
    M = mk.shape[0] // B
    nt = S // tm
    row = pl.BlockSpec((tm, D), lambda b, i: (b * nt + i, 0))
    memspec = pl.BlockSpec((M, D), lambda b, i: (b, 0))
    hd = D // MEM_HEADS
    return pl.pallas_call(
        functools.partial(_mixmem_kernel, heads=MEM_HEADS, q_scale=1.0 / math.sqrt(hd)),
        grid=(B, nt),
        in_specs=[row, row, row, _const_spec(wo1.shape), _const_spec(wo2.shape), _const_spec(nw.shape),
                  _const_spec(wq.shape), memspec, memspec, _const_spec(wo.shape)],
        out_specs=row,
        out_shape=jax.ShapeDtypeStruct((T, D), _F32),
        compiler_params=_params("arbitrary", "arbitrary"),
        name="mix_mem",
    )(x2d, ya, yb, wo1, wo2, nw, wq, mk, mv, wo)


def _ffn_kernel(x_ref, halo_ref, nw_ref, wup_ref, cw_ref, cb_ref, wdn_ref, fw_ref,
                o_ref, acc_ref, h_ref, ua_ref, ub_ref, aa_ref, ab_ref, *, tm, fc, dff, final_norm):
    i = pl.program_id(1)
    x = x_ref[...]
    keep = (i > 0).astype(_F32)
    h_ref[...] = jnp.concatenate(
        [_rms(halo_ref[...], nw_ref[...]) * keep, _rms(x, nw_ref[...])], axis=0).astype(_BF)
    acc_ref[...] = jnp.zeros_like(acc_ref)

    ubuf = (ua_ref, ub_ref)
    abuf = (aa_ref, ab_ref)
    starts = list(range(0, dff, fc))
    widths = [min(fc, dff - s0) for s0 in starts]
    nchunk = len(starts)

    def cols(c, base):
        return slice(base + starts[c], base + starts[c] + widths[c])

    def up(c):
        w = widths[c]
        ubuf[c % 2][0, :, :w] = _dot(h_ref[...], wup_ref[:, cols(c, 0)])
        ubuf[c % 2][1, :, :w] = _dot(h_ref[...], wup_ref[:, cols(c, dff)])

    def conv(u, cw, cb):
        out = cb + cw[FFN_CONV - 1:FFN_CONV, :] * u
        for kk in range(1, FFN_CONV):
            out = out + cw[FFN_CONV - 1 - kk:FFN_CONV - kk, :] * pltpu.roll(u, kk, 0)
        return out[FFN_HALO:]

    def gate(c):
        w = widths[c]
        g = conv(ubuf[c % 2][0, :, :w], cw_ref[:, cols(c, 0)], cb_ref[:, cols(c, 0)])
        val = conv(ubuf[c % 2][1, :, :w], cw_ref[:, cols(c, dff)], cb_ref[:, cols(c, dff)])
        abuf[c % 2][:, :w] = (_silu(g) * val).astype(_BF)

    def down(c):
        acc_ref[...] += _dot(abuf[c % 2][:, :widths[c]], wdn_ref[cols(c, 0), :])

    up(0)
    if nchunk > 1:
        up(1)
    gate(0)
    for c in range(nchunk):
        down(c)
        if c + 1 < nchunk:
            gate(c + 1)
        if c + 2 < nchunk:
            up(c + 2)
    y = x + acc_ref[...]
    o_ref[...] = _rms(y, fw_ref[...]) if final_norm else y


def _ffn(x2d, nw, wup, cw, cb, wdn, fw, *, B, S, tm, final_norm):
    T, D = x2d.shape
    dff = wdn.shape[0]
    fc = FFN_CHUNK
    assert dff % LANES == 0 and tm % FFN_HALO == 0
    nt = S // tm
    hb = tm // FFN_HALO
    row = pl.BlockSpec((tm, D), lambda b, i: (b * nt + i, 0))
    halo = pl.BlockSpec((FFN_HALO, D), lambda b, i: (jnp.maximum((b * nt + i) * hb - 1, 0), 0))
    return pl.pallas_call(
        functools.partial(_ffn_kernel, tm=tm, fc=fc, dff=dff, final_norm=final_norm),
        grid=(B, nt),
        in_specs=[row, halo] + [_const_spec(a.shape) for a in (nw, wup, cw, cb, wdn, fw)],
        out_specs=row,
        out_shape=jax.ShapeDtypeStruct((T, D), _F32),
        scratch_shapes=[pltpu.VMEM((tm, D), _F32),
                        pltpu.VMEM((tm + FFN_HALO, D), _BF),
                        pltpu.VMEM((2, tm + FFN_HALO, fc), _F32),
                        pltpu.VMEM((2, tm + FFN_HALO, fc), _F32),
                        pltpu.VMEM((tm, fc), _BF),
                        pltpu.VMEM((tm, fc), _BF)],
        compiler_params=_params("arbitrary", "arbitrary"),
        name="ffn",
    )(x2d, x2d, nw, wup, cw, cb, wdn, fw)


def _pad_lanes(a):
    return jnp.pad(a, ((0, 0), (0, LANES - a.shape[1])))


def kernel(x, mem, norm_mix_w, w_in, conv_ssd_w, conv_ssd_b, dt_bias, a_log, d_skip, ssd_norm_w, sb_norm_w, w_out, norm_mem_w, norm_memkv_w, w_mq, w_mk, w_mv, w_mo, norm_ffn_w, w_up, conv_ffn_w, conv_ffn_b, w_down, norm_final_w):
    B, S, D = x.shape
    T = B * S
    depth = w_in.shape[0]
    H = SSD_HEADS
    DI = ssd_norm_w.shape[1]
    C = conv_ssd_w.shape[2]
    W = sb_norm_w.shape[1]
    o1, o2 = DI, DI + C
    o3 = o2 + H
    o4, o5 = o3 + W, o3 + 2 * W
    tm = min(512, S)

    x2d = x.reshape(T, D)
    mem2d = mem.reshape(B * mem.shape[1], D)
    for l in range(depth):
        wi = w_in[l]
        bf = lambda a: a.astype(_BF)
        z, xbc, dt_raw, q, k, v = _in_proj(
            x2d, norm_mix_w[l][None], bf(wi[:, :o1]), bf(wi[:, o1:o2]), bf(_pad_lanes(wi[:, o2:o3])),
            bf(wi[:, o3:o4]), bf(wi[:, o4:o5]), bf(wi[:, o5:]), tm=min(512, S))
        y_ssd = _ssd(z, xbc, dt_raw, conv_ssd_w[l], conv_ssd_b[l][None], _pad_lanes(dt_bias[l][None]),
                     _pad_lanes(a_log[l][None]), jnp.repeat(d_skip[l], DI // H)[None], ssd_norm_w[l][None],
                     B=B, S=S)
        y_sb = _sb_attention(q, k, v, sb_norm_w[l][None], B=B, S=S)
        mk, mv = _mem_kv(mem2d, norm_memkv_w[l][None], bf(w_mk[l]), bf(w_mv[l]))
        x2d = _mix_mem(x2d, y_ssd, y_sb, bf(w_out[l][:DI]), bf(w_out[l][DI:]), norm_mem_w[l][None],
                       bf(w_mq[l]), mk, mv, bf(w_mo[l]), B=B, S=S, tm=tm)
        x2d = _ffn(x2d, norm_ffn_w[l][None], bf(w_up[l]), conv_ffn_w[l], conv_ffn_b[l][None], bf(w_down[l]),
                   norm_final_w[None], B=B, S=S, tm=tm, final_norm=(l == depth - 1))
    return x2d.reshape(B, S, D)
```

```python
import functools
import math

import jax
import jax.numpy as jnp
from jax import lax
from jax.experimental import pallas as pl
from jax.experimental.pallas import tpu as pltpu

EPS = 1e-6
LANES = 128
SUBLANES = 8
VMEM_LIMIT = 56 * 1024 * 1024

SSD_HEADS = 16
SSD_GROUPS = 2
SSD_STATE = 128
SSD_CONV = 4
SSD_CHUNK = 128
SB_HEAD_DIM = 64
SB_TILE = 128
SB_PAIRS_PER_STEP = 8
SB_TOP_ROWS = 48
SB_QTILES_PER_STEP = 8
MEM_HEADS = 4
FFN_CONV = 3
FFN_CHUNK = 768
FFN_HALO = 16

SB_SKIP_LOG = -110.0

_BF = jnp.bfloat16
_F32 = jnp.float32


def _dot(a, b):
    return jnp.dot(a, b, preferred_element_type=_F32)


def _dot_nt(a, b):
    return lax.dot_general(a, b, (((1,), (1,)), ((), ())), preferred_element_type=_F32)


def _rms(x, w):
    return x * lax.rsqrt(jnp.mean(x * x, axis=-1, keepdims=True) + EPS) * w


def _softplus(x):
    return jnp.maximum(x, 0.0) + jnp.log(1.0 + jnp.exp(-jnp.abs(x)))


def _silu(x):
    return x * jax.nn.sigmoid(x)


def _params(*sem):
    return pltpu.CompilerParams(dimension_semantics=sem, vmem_limit_bytes=VMEM_LIMIT)


def _const_spec(shape):
    nd = len(shape)
    return pl.BlockSpec(shape, lambda *_: (0,) * nd, pipeline_mode=pl.Buffered(1))


def _inproj_kernel(x_ref, nw_ref, wz_ref, wxbc_ref, wdt_ref, wq_ref, wk_ref, wv_ref,
                   z_ref, xbc_ref, dt_ref, q_ref, k_ref, v_ref, *, q_scale):
    hb = _rms(x_ref[...], nw_ref[...]).astype(_BF)
    z_ref[...] = _dot(hb, wz_ref[...])
    xbc_ref[...] = _dot(hb, wxbc_ref[...])
    dt_ref[...] = _dot(hb, wdt_ref[...])
    q_ref[...] = (_dot(hb, wq_ref[...]) * q_scale).astype(_BF)
    k_ref[...] = _dot(hb, wk_ref[...]).astype(_BF)
    v_ref[...] = _dot(hb, wv_ref[...]).astype(_BF)


def _in_proj(x2d, nw, wz, wxbc, wdt, wq, wk, wv, *, tm):
    T, D = x2d.shape
    row = lambda n: pl.BlockSpec((tm, n), lambda i: (i, 0))
    ws = (wz, wxbc, wdt, wq, wk, wv)
    out_dtypes = (_F32, _F32, _F32, _BF, _BF, _BF)
    return pl.pallas_call(
        functools.partial(_inproj_kernel, q_scale=1.0 / math.sqrt(SB_HEAD_DIM)),
        grid=(T // tm,),
        in_specs=[row(D), _const_spec(nw.shape)] + [_const_spec(w.shape) for w in ws],
        out_specs=[row(w.shape[1]) for w in ws],
        out_shape=[jax.ShapeDtypeStruct((T, w.shape[1]), dt) for w, dt in zip(ws, out_dtypes)],
        compiler_params=_params("arbitrary"),
        name="in_proj",
    )(x2d, nw, *ws)


def _split3(v):
    v1 = v.astype(_BF)
    r1 = v - v1.astype(_F32)
    v2 = r1.astype(_BF)
    v3 = (r1 - v2.astype(_F32)).astype(_BF)
    return v1, v2, v3


def _ssd_kernel(z_ref, xbc_ref, dt_ref, cw_ref, cb_ref, dtb_ref, alog_ref, dsk_ref, nw_ref,
                y_ref, tail_ref, state_ref, *, L, H, P, G, N):
    DI = H * P
    HG = H // G
    GW = DI // G
    c = pl.program_id(1)

    @pl.when(c == 0)
    def _():
        tail_ref[...] = jnp.zeros_like(tail_ref)
        state_ref[...] = jnp.zeros_like(state_ref)

    xbc = xbc_ref[...]
    ext = jnp.concatenate([tail_ref[...], xbc], axis=0)
    tail_ref[...] = xbc[L - SUBLANES:, :]
    conv = cb_ref[...] + cw_ref[SSD_CONV - 1:SSD_CONV, :] * xbc
    for kk in range(1, SSD_CONV):
        conv = conv + cw_ref[SSD_CONV - 1 - kk:SSD_CONV - kk, :] * pltpu.roll(ext, kk, 0)[SUBLANES:]
    act = _silu(conv)
    xs = act[:, :DI]
    Bm = act[:, DI:DI + G * N]
    Cm = act[:, DI + G * N:]

    dt = _softplus(dt_ref[...] + dtb_ref[...])
    a = dt * (-jnp.exp(alog_ref[...]))
    ri = lax.broadcasted_iota(jnp.int32, (L, L), 0)
    ci = lax.broadcasted_iota(jnp.int32, (L, L), 1)
    tril = ri >= ci
    tril_bf = jnp.where(tril, 1.0, 0.0).astype(_BF)
    a1, a2, a3 = _split3(a)
    a_cs = _dot(tril_bf, a1) + _dot(tril_bf, a2) + _dot(tril_bf, a3)
    a_last = a_cs[L - 1:L, :]
    a_cs_t = a_cs.T
    dt_t = dt.T
    w_col = jnp.exp(a_last - a_cs) * dt
    e_col = jnp.exp(a_cs)
    c_dec = jnp.exp(a_last)

    lane = lax.broadcasted_iota(jnp.int32, (1, 2 * P), 1)
    lo = lane < P

    for g in range(G):
        B_g = Bm[:, g * N:(g + 1) * N]
        C_g = Cm[:, g * N:(g + 1) * N].astype(_BF)
        cb = _dot_nt(C_g, B_g.astype(_BF))
        b_t = B_g.T.astype(_BF)
        ys = []
        for pr in range(HG // 2):
            h0 = g * HG + 2 * pr
            h1 = h0 + 1
            cols = slice(h0 * P, (h0 + 2) * P)
            xs2 = xs[:, cols]
            pick = lambda v: jnp.where(lo, v[:, h0:h0 + 1], v[:, h1:h1 + 1])
            prev2 = state_ref[:, cols]
            y_off = _dot(C_g, prev2.astype(_BF)) * pick(e_col)
            ms = []
            for hh in (h0, h1):
                seg = a_cs[:, hh:hh + 1] - a_cs_t[hh:hh + 1, :]
                lmat = jnp.exp(jnp.where(tril, seg, -jnp.inf))
                ms.append((cb * lmat * dt_t[hh:hh + 1, :]).astype(_BF))
            xs_lo = jnp.where(lo, xs2, 0.0).astype(_BF)
            xs_hi = jnp.where(lo, 0.0, xs2).astype(_BF)
            y_diag = _dot(jnp.concatenate(ms, axis=1), jnp.concatenate([xs_lo, xs_hi], axis=0))
            new_state = _dot(b_t, (xs2 * pick(w_col)).astype(_BF))
            state_ref[:, cols] = prev2 * pick(c_dec) + new_state
            y2 = y_diag + y_off + xs2 * dsk_ref[:, cols]
            ys.append(y2 * _silu(z_ref[:, cols]))
        yg = jnp.concatenate(ys, axis=1)
        y_ref[:, g * GW:(g + 1) * GW] = _rms(yg, nw_ref[:, g * GW:(g + 1) * GW]).astype(_BF)


def _ssd(z, xbc, dt_raw, cw, cb, dtb, alog, dsk, nw, *, B, S):
    T, DI = z.shape
    C = xbc.shape[1]
    L = SSD_CHUNK
    H, G, N = SSD_HEADS, SSD_GROUPS, SSD_STATE
    P = DI // H
    assert 2 * P == LANES and H % (2 * G) == 0 and S % L == 0
    nc = S // L
    row = lambda n: pl.BlockSpec((L, n), lambda b, c: (b * nc + c, 0))
    consts = (cw, cb, dtb, alog, dsk, nw)
    return pl.pallas_call(
        functools.partial(_ssd_kernel, L=L, H=H, P=P, G=G, N=N),
        grid=(B, nc),
        in_specs=[row(DI), row(C), row(LANES)] + [_const_spec(a.shape) for a in consts],
        out_specs=row(DI),
        out_shape=jax.ShapeDtypeStruct((T, DI), _BF),
        scratch_shapes=[pltpu.VMEM((SUBLANES, C), _F32), pltpu.VMEM((N, DI), _F32)],
        compiler_params=_params("arbitrary", "arbitrary"),
        name="ssd",
    )(z, xbc, dt_raw, *consts)


def _sb_kernel(q_ref, k_ref, v_ref, nw_ref, o_ref, acc_ref, r_ref, z_ref, p_ref, *, tq, hd, npp, nqb):
    lane = lax.broadcasted_iota(jnp.int32, (1, 2 * hd), 1)
    lo = lane < hd
    nh = 2 * npp
    ri = lax.broadcasted_iota(jnp.int32, (tq, tq), 0)
    ci = lax.broadcasted_iota(jnp.int32, (tq, tq), 1)
    tri = jnp.where(ri >= ci, 1.0, 0.0).astype(_BF)
    tri2 = jnp.concatenate([tri, tri], axis=0)
    rs = lax.broadcasted_iota(jnp.int32, (nh * tq, tq), 0) & (tq - 1)
    cs = lax.broadcasted_iota(jnp.int32, (nh * tq, tq), 1)
    causal = cs < rs
    first_tile = pl.program_id(2) * nqb

    def key_start(j):
        return pl.multiple_of(jnp.maximum(j, 0) * tq, tq)

    def weights(mode):
        nr = SB_TOP_ROWS if mode == "top" else tq

        def head_rows(ref):
            if mode != "top":
                return ref[...]
            return jnp.concatenate([ref[h * tq:h * tq + nr, :] for h in range(nh)], axis=0)

        z = head_rows(z_ref)
        if mode == "diag":
            z = jnp.where(causal, z, -jnp.inf)
        lnb = -_softplus(z)
        r = head_rows(r_ref)
        one = jnp.int32(1)
        if mode == "diag":
            live_top, live_rest = one, one
        else:
            tot = r[:, 0:1] + jnp.sum(lnb, axis=-1, keepdims=True)
            if mode == "top":
                live_top, live_rest = (jnp.max(tot) > SB_SKIP_LOG).astype(jnp.int32), jnp.int32(0)
            else:
                pick = lambda lo_row, hi_row: jnp.concatenate(
                    [tot[h * tq + lo_row:h * tq + hi_row] for h in range(nh)], axis=0)
                live_top = (jnp.max(pick(0, SB_TOP_ROWS)) > SB_SKIP_LOG).astype(jnp.int32)
                live_rest = (jnp.max(pick(SB_TOP_ROWS, tq)) > SB_SKIP_LOG).astype(jnp.int32)
        hi = lnb.astype(_BF)
        lw = (lnb - hi.astype(_F32)).astype(_BF)
        cum = _dot(jnp.concatenate([hi, lw], axis=1), tri2)
        a = jnp.exp(z + cum + r)
        r_new = r + cum[:, 0:1]
        if mode == "top":
            for h in range(nh):
                r_ref[h * tq:h * tq + nr, :] = r_new[h * nr:(h + 1) * nr]
        else:
            r_ref[...] = r_new
        a = a.astype(_BF)
        for p in range(npp):
            p_ref[p, 0:nr, :] = jnp.concatenate(
                [a[2 * p * nr:(2 * p + 1) * nr], a[(2 * p + 1) * nr:(2 * p + 2) * nr]], axis=1)
            if mode == "top":
                p_ref[p, nr:tq, :] = jnp.zeros((tq - nr, 2 * tq), _BF)
        return live_top, live_rest

    def apply_weights(j):
        start = key_start(j)
        for p in range(npp):
            cols = slice(p * LANES, (p + 1) * LANES)
            v2 = v_ref[pl.ds(start, tq), cols]
            zv = jnp.zeros_like(v2)
            vv = jnp.concatenate([jnp.where(lo, v2, zv), jnp.where(lo, zv, v2)], axis=0)
            acc_ref[:, cols] += _dot(p_ref[p], vv)

    def finish(qb):
        rows = pl.ds(pl.multiple_of(qb * tq, tq), tq)
        for p in range(npp):
            cols = slice(p * LANES, (p + 1) * LANES)
            acc = acc_ref[:, cols]
            sq = acc * acc
            s_lo = jnp.sum(jnp.where(lo, sq, 0.0), axis=-1, keepdims=True)
            s_hi = jnp.sum(jnp.where(lo, 0.0, sq), axis=-1, keepdims=True)
            inv = lax.rsqrt(jnp.where(lo, s_lo, s_hi) * (1.0 / hd) + EPS)
            o_ref[rows, cols] = (acc * inv * nw_ref[:, cols]).astype(_BF)

    acc_ref[...] = jnp.zeros_like(acc_ref)
    p_ref[...] = jnp.zeros_like(p_ref)

    def query_tile(qb, pending):
        i = first_tile + qb
        rows = pl.ds(pl.multiple_of(qb * tq, tq), tq)
        qcat = []
        for p in range(npp):
            q2 = q_ref[rows, p * LANES:(p + 1) * LANES]
            zq = jnp.zeros_like(q2)
            qcat.append(jnp.concatenate([jnp.where(lo, q2, zq), jnp.where(lo, zq, q2)], axis=0))

        def scores(j):
            start = key_start(j)
            return jnp.concatenate(
                [_dot_nt(qcat[p], k_ref[pl.ds(start, tq), p * LANES:(p + 1) * LANES]) for p in range(npp)],
                axis=0)

        z_first = scores(i)
        z_next = scores(i - 1)
        apply_weights(pending)
        finish(jnp.maximum(qb - 1, 0))
        acc_ref[...] = jnp.zeros_like(acc_ref)
        r_ref[...] = jnp.zeros_like(r_ref)
        z_ref[...] = z_first
        live_top0, live_rest0 = weights("diag")
        z_ref[...] = z_next

        def cond(carry):
            j, live_top, live_rest = carry
            return jnp.logical_and(j >= 0, (live_top + live_rest) > 0)

        def body(carry):
            j, _, live_rest = carry

            def step(mode):
                def run():
                    apply_weights(j + 1)
                    z_next = scores(j - 1)
                    live = weights(mode)
                    z_ref[...] = z_next
                    return live
                return run

            live_top, live_rest = lax.cond(live_rest > 0, step("full"), step("top"))
            return j - 1, live_top, live_rest

        j_end, _, _ = lax.while_loop(cond, body, (i - 1, live_top0, live_rest0))
        return j_end + 1

    pending = lax.fori_loop(0, nqb, query_tile, jnp.int32(0))
    apply_weights(pending)
    finish(nqb - 1)


def _sb_attention(q, k, v, nw, *, B, S):
    T, W = q.shape
    hd = SB_HEAD_DIM
    tq = SB_TILE
    npp = SB_PAIRS_PER_STEP
    cw = npp * LANES
    assert 2 * hd == LANES and tq == LANES and S % tq == 0 and W % cw == 0
    nqb = math.gcd(SB_QTILES_PER_STEP, S // tq)
    nq = S // (tq * nqb)
    k3 = k.reshape(B, S, W)
    v3 = v.reshape(B, S, W)
    return pl.pallas_call(
        functools.partial(_sb_kernel, tq=tq, hd=hd, npp=npp, nqb=nqb),
        grid=(B, W // cw, nq),
        in_specs=[
            pl.BlockSpec((nqb * tq, cw), lambda b, p, i: (b * nq + i, p)),
            pl.BlockSpec((None, S, cw), lambda b, p, i: (b, 0, p), pipeline_mode=pl.Buffered(1)),
            pl.BlockSpec((None, S, cw), lambda b, p, i: (b, 0, p), pipeline_mode=pl.Buffered(1)),
            pl.BlockSpec((1, cw), lambda b, p, i: (0, p)),
        ],
        out_specs=pl.BlockSpec((nqb * tq, cw), lambda b, p, i: (b * nq + i, p)),
        out_shape=jax.ShapeDtypeStruct((T, W), _BF),
        scratch_shapes=[pltpu.VMEM((tq, cw), _F32), pltpu.VMEM((2 * npp * tq, tq), _F32),
                        pltpu.VMEM((2 * npp * tq, tq), _F32), pltpu.VMEM((npp, tq, 2 * tq), _BF)],
        compiler_params=_params("arbitrary", "arbitrary", "arbitrary"),
        name="sb_attn",
    )(q, k3, v3, nw)


def _memkv_kernel(m_ref, nw_ref, wk_ref, wv_ref, k_ref, v_ref):
    mb = _rms(m_ref[...], nw_ref[...]).astype(_BF)
    k_ref[...] = _dot(mb, wk_ref[...]).astype(_BF)
    v_ref[...] = _dot(mb, wv_ref[...]).astype(_BF)


def _mem_kv(mem2d, nw, wk, wv):
    R, D = mem2d.shape
    return pl.pallas_call(
        _memkv_kernel,
        grid=(1,),
        in_specs=[_const_spec(a.shape) for a in (mem2d, nw, wk, wv)],
        out_specs=[_const_spec((R, D))] * 2,
        out_shape=[jax.ShapeDtypeStruct((R, D), _BF)] * 2,
        compiler_params=_params("arbitrary"),
        name="mem_kv",
    )(mem2d, nw, wk, wv)


def _mixmem_kernel(x_ref, ya_ref, yb_ref, wo1_ref, wo2_ref, nw_ref, wq_ref, mk_ref, mv_ref, wo_ref,
                   o_ref, *, heads, q_scale):
    x1 = x_ref[...] + _dot(ya_ref[...], wo1_ref[...]) + _dot(yb_ref[...], wo2_ref[...])
    hb = _rms(x1, nw_ref[...]).astype(_BF)
    q = (_dot(hb, wq_ref[...]) * q_scale).astype(_BF)
    D = q.shape[1]
    hd = D // heads
    outs = []
    for h in range(heads):
        cols = slice(h * hd, (h + 1) * hd)
        sc = _dot_nt(q[:, cols], mk_ref[:, cols])
        sc = sc - jnp.max(sc, axis=-1, keepdims=True)
        e = jnp.exp(sc)
        p = e / jnp.sum(e, axis=-1, keepdims=True)
        outs.append(_dot(p.astype(_BF), mv_ref[:, cols]).astype(_BF))
    o = jnp.concatenate(outs, axis=1)
    o_ref[...] = x1 + _dot(o, wo_ref[...])


def _mix_mem(x2d, ya, yb, wo1, wo2, nw, wq, mk, mv, wo, *, B, S, tm):
    T, D = x2d.shape
    M = mk.shape[0] // B
    nt = S // tm
    row = pl.BlockSpec((tm, D), lambda b, i: (b * nt + i, 0))
    memspec = pl.BlockSpec((M, D), lambda b, i: (b, 0))
    hd = D // MEM_HEADS
    return pl.pallas_call(
        functools.partial(_mixmem_kernel, heads=MEM_HEADS, q_scale=1.0 / math.sqrt(hd)),
        grid=(B, nt),
        in_specs=[row, row, row, _const_spec(wo1.shape), _const_spec(wo2.shape), _const_spec(nw.shape),
                  _const_spec(wq.shape), memspec, memspec, _const_spec(wo.shape)],
        out_specs=row,
        out_shape=jax.ShapeDtypeStruct((T, D), _F32),
        compiler_params=_params("arbitrary", "arbitrary"),
        name="mix_mem",
    )(x2d, ya, yb, wo1, wo2, nw, wq, mk, mv, wo)


def _ffn_kernel(x_ref, halo_ref, nw_ref, wup_ref, cw_ref, cb_ref, wdn_ref, fw_ref,
                o_ref, acc_ref, h_ref, ua_ref, ub_ref, aa_ref, ab_ref, *, tm, fc, dff, final_norm):
    i = pl.program_id(1)
    x = x_ref[...]
    keep = (i > 0).astype(_F32)
    h_ref[...] = jnp.concatenate(
        [_rms(halo_ref[...], nw_ref[...]) * keep, _rms(x, nw_ref[...])], axis=0).astype(_BF)
    acc_ref[...] = jnp.zeros_like(acc_ref)

    ubuf = (ua_ref, ub_ref)
    abuf = (aa_ref, ab_ref)
    starts = list(range(0, dff, fc))
    widths = [min(fc, dff - s0) for s0 in starts]
    nchunk = len(starts)

    def cols(c, base):
        return slice(base + starts[c], base + starts[c] + widths[c])

    def up(c):
        w = widths[c]
        ubuf[c % 2][0, :, :w] = _dot(h_ref[...], wup_ref[:, cols(c, 0)])
        ubuf[c % 2][1, :, :w] = _dot(h_ref[...], wup_ref[:, cols(c, dff)])

    def conv(u, cw, cb):
        out = cb + cw[FFN_CONV - 1:FFN_CONV, :] * u
        for kk in range(1, FFN_CONV):
            out = out + cw[FFN_CONV - 1 - kk:FFN_CONV - kk, :] * pltpu.roll(u, kk, 0)
        return out[FFN_HALO:]

    def gate(c):
        w = widths[c]
        g = conv(ubuf[c % 2][0, :, :w], cw_ref[:, cols(c, 0)], cb_ref[:, cols(c, 0)])
        val = conv(ubuf[c % 2][1, :, :w], cw_ref[:, cols(c, dff)], cb_ref[:, cols(c, dff)])
        abuf[c % 2][:, :w] = (_silu(g) * val).astype(_BF)

    def down(c):
        acc_ref[...] += _dot(abuf[c % 2][:, :widths[c]], wdn_ref[cols(c, 0), :])

    up(0)
    if nchunk > 1:
        up(1)
    gate(0)
    for c in range(nchunk):
        down(c)
        if c + 1 < nchunk:
            gate(c + 1)
        if c + 2 < nchunk:
            up(c + 2)
    y = x + acc_ref[...]
    o_ref[...] = _rms(y, fw_ref[...]) if final_norm else y


def _ffn(x2d, nw, wup, cw, cb, wdn, fw, *, B, S, tm, final_norm):
    T, D = x2d.shape
    dff = wdn.shape[0]
    fc = FFN_CHUNK
    assert dff % LANES == 0 and tm % FFN_HALO == 0
    nt = S // tm
    hb = tm // FFN_HALO
    row = pl.BlockSpec((tm, D), lambda b, i: (b * nt + i, 0))
    halo = pl.BlockSpec((FFN_HALO, D), lambda b, i: (jnp.maximum((b * nt + i) * hb - 1, 0), 0))
    return pl.pallas_call(
        functools.partial(_ffn_kernel, tm=tm, fc=fc, dff=dff, final_norm=final_norm),
        grid=(B, nt),
        in_specs=[row, halo] + [_const_spec(a.shape) for a in (nw, wup, cw, cb, wdn, fw)],
        out_specs=row,
        out_shape=jax.ShapeDtypeStruct((T, D), _F32),
        scratch_shapes=[pltpu.VMEM((tm, D), _F32),
                        pltpu.VMEM((tm + FFN_HALO, D), _BF),
                        pltpu.VMEM((2, tm + FFN_HALO, fc), _F32),
                        pltpu.VMEM((2, tm + FFN_HALO, fc), _F32),
                        pltpu.VMEM((tm, fc), _BF),
                        pltpu.VMEM((tm, fc), _BF)],
        compiler_params=_params("arbitrary", "arbitrary"),
        name="ffn",
    )(x2d, x2d, nw, wup, cw, cb, wdn, fw)


def _pad_lanes(a):
    return jnp.pad(a, ((0, 0), (0, LANES - a.shape[1])))


def kernel(x, mem, norm_mix_w, w_in, conv_ssd_w, conv_ssd_b, dt_bias, a_log, d_skip, ssd_norm_w, sb_norm_w, w_out, norm_mem_w, norm_memkv_w, w_mq, w_mk, w_mv, w_mo, norm_ffn_w, w_up, conv_ffn_w, conv_ffn_b, w_down, norm_final_w):
    B, S, D = x.shape
    T = B * S
    depth = w_in.shape[0]
    H = SSD_HEADS
    DI = ssd_norm_w.shape[1]
    C = conv_ssd_w.shape[2]
    W = sb_norm_w.shape[1]
    o1, o2 = DI, DI + C
    o3 = o2 + H
    o4, o5 = o3 + W, o3 + 2 * W
    tm = min(512, S)

    x2d = x.reshape(T, D)
    mem2d = mem.reshape(B * mem.shape[1], D)
    for l in range(depth):
        wi = w_in[l]
        bf = lambda a: a.astype(_BF)
        z, xbc, dt_raw, q, k, v = _in_proj(
            x2d, norm_mix_w[l][None], bf(wi[:, :o1]), bf(wi[:, o1:o2]), bf(_pad_lanes(wi[:, o2:o3])),
            bf(wi[:, o3:o4]), bf(wi[:, o4:o5]), bf(wi[:, o5:]), tm=min(512, S))
        y_ssd = _ssd(z, xbc, dt_raw, conv_ssd_w[l], conv_ssd_b[l][None], _pad_lanes(dt_bias[l][None]),
                     _pad_lanes(a_log[l][None]), jnp.repeat(d_skip[l], DI // H)[None], ssd_norm_w[l][None],
                     B=B, S=S)
        y_sb = _sb_attention(q, k, v, sb_norm_w[l][None], B=B, S=S)
        mk, mv = _mem_kv(mem2d, norm_memkv_w[l][None], bf(w_mk[l]), bf(w_mv[l]))
        x2d = _mix_mem(x2d, y_ssd, y_sb, bf(w_out[l][:DI]), bf(w_out[l][DI:]), norm_mem_w[l][None],
                       bf(w_mq[l]), mk, mv, bf(w_mo[l]), B=B, S=S, tm=min(1024, S))
        x2d = _ffn(x2d, norm_ffn_w[l][None], bf(w_up[l]), conv_ffn_w[l], conv_ffn_b[l][None], bf(w_down[l]),
                   norm_final_w[None], B=B, S=S, tm=tm, final_norm=(l == depth - 1))
    return x2d.reshape(B, S, D)
```

```python
import functools
import math

import jax
import jax.numpy as jnp
from jax import lax
from jax.experimental import pallas as pl
from jax.experimental.pallas import tpu as pltpu

EPS = 1e-6
LANES = 128
SUBLANES = 8
VMEM_LIMIT = 56 * 1024 * 1024

SSD_HEADS = 16
SSD_GROUPS = 2
SSD_STATE = 128
SSD_CONV = 4
SSD_CHUNK = 128
SSD_CHUNKS_PER_STEP = 4
SB_HEAD_DIM = 64
SB_TILE = 128
SB_PAIRS_PER_STEP = 8
SB_TOP_ROWS = 48
SB_QTILES_PER_STEP = 8
MEM_HEADS = 4
FFN_CONV = 3
FFN_CHUNK = 768
FFN_HALO = 16

SB_SKIP_LOG = -110.0

_BF = jnp.bfloat16
_F32 = jnp.float32


def _dot(a, b):
    return jnp.dot(a, b, preferred_element_type=_F32)


def _dot_nt(a, b):
    return lax.dot_general(a, b, (((1,), (1,)), ((), ())), preferred_element_type=_F32)


def _rms(x, w):
    return x * lax.rsqrt(jnp.mean(x * x, axis=-1, keepdims=True) + EPS) * w


def _softplus(x):
    return jnp.maximum(x, 0.0) + jnp.log(1.0 + jnp.exp(-jnp.abs(x)))


def _silu(x):
    return x * jax.nn.sigmoid(x)


def _params(*sem):
    return pltpu.CompilerParams(dimension_semantics=sem, vmem_limit_bytes=VMEM_LIMIT)


def _const_spec(shape):
    nd = len(shape)
    return pl.BlockSpec(shape, lambda *_: (0,) * nd, pipeline_mode=pl.Buffered(1))


def _inproj_kernel(x_ref, nw_ref, w_ref, z_ref, xbc_ref, dt_ref, q_ref, k_ref, v_ref,
                   wq_ref, wk_ref, wv_ref, *, q_scale, offs):
    o1, o2, o3, o4, o5, o6 = offs

    @pl.when(pl.program_id(0) == 0)
    def _():
        wq_ref[...] = w_ref[:, o3:o4]
        wk_ref[...] = w_ref[:, o4:o5]
        wv_ref[...] = w_ref[:, o5:o6]

    hb = _rms(x_ref[...], nw_ref[...]).astype(_BF)
    z_ref[...] = _dot(hb, w_ref[:, 0:o1])
    xbc_ref[...] = _dot(hb, w_ref[:, o1:o2])
    dt_ref[...] = _dot(hb, w_ref[:, o2:o2 + LANES])
    q_ref[...] = (_dot(hb, wq_ref[...]) * q_scale).astype(_BF)
    k_ref[...] = _dot(hb, wk_ref[...]).astype(_BF)
    v_ref[...] = _dot(hb, wv_ref[...]).astype(_BF)


def _in_proj(x2d, nw, w, offs, *, tm):
    T, D = x2d.shape
    o1, o2, o3, o4, o5, o6 = offs
    assert o1 % LANES == 0 and o2 % LANES == 0 and o2 + LANES <= o6
    row = lambda n: pl.BlockSpec((tm, n), lambda i: (i, 0))
    widths = (o1, o2 - o1, LANES, o4 - o3, o5 - o4, o6 - o5)
    out_dtypes = (_F32, _F32, _F32, _BF, _BF, _BF)
    return pl.pallas_call(
        functools.partial(_inproj_kernel, q_scale=1.0 / math.sqrt(SB_HEAD_DIM), offs=offs),
        grid=(T // tm,),
        in_specs=[row(D), _const_spec(nw.shape), _const_spec(w.shape)],
        out_specs=[row(n) for n in widths],
        out_shape=[jax.ShapeDtypeStruct((T, n), dt) for n, dt in zip(widths, out_dtypes)],
        scratch_shapes=[pltpu.VMEM((D, n), _BF) for n in widths[3:]],
        compiler_params=_params("arbitrary"),
        name="in_proj",
    )(x2d, nw, w)


def _split3(v):
    v1 = v.astype(_BF)
    r1 = v - v1.astype(_F32)
    v2 = r1.astype(_BF)
    v3 = (r1 - v2.astype(_F32)).astype(_BF)
    return v1, v2, v3


def _ssd_kernel(z_ref, xbc_ref, dt_ref, cw_ref, cb_ref, dtb_ref, alog_ref, dsk_ref, nw_ref,
                y_ref, tail_ref, state_ref, *, L, H, P, G, N, nsub):
    DI = H * P
    HG = H // G
    GW = DI // G
    R = nsub * L
    c = pl.program_id(1)

    @pl.when(c == 0)
    def _():
        tail_ref[...] = jnp.zeros_like(tail_ref)
        state_ref[...] = jnp.zeros_like(state_ref)

    xbc = xbc_ref[...]
    ext = jnp.concatenate([tail_ref[...], xbc], axis=0)
    tail_ref[...] = xbc[R - SUBLANES:, :]
    conv = cb_ref[...] + cw_ref[SSD_CONV - 1:SSD_CONV, :] * xbc
    for kk in range(1, SSD_CONV):
        conv = conv + cw_ref[SSD_CONV - 1 - kk:SSD_CONV - kk, :] * pltpu.roll(ext, kk, 0)[SUBLANES:]
    act_all = _silu(conv)
    dt_all = _softplus(dt_ref[...] + dtb_ref[...])
    a_all = dt_all * (-jnp.exp(alog_ref[...]))

    ri = lax.broadcasted_iota(jnp.int32, (L, L), 0)
    ci = lax.broadcasted_iota(jnp.int32, (L, L), 1)
    tril = ri >= ci
    tril_bf = jnp.where(tril, 1.0, 0.0).astype(_BF)
    lane = lax.broadcasted_iota(jnp.int32, (1, 2 * P), 1)
    lo = lane < P

    for sc in range(nsub):
        rows = slice(sc * L, (sc + 1) * L)
        act = act_all[rows]
        xs = act[:, :DI]
        Bm = act[:, DI:DI + G * N]
        Cm = act[:, DI + G * N:]
        dt = dt_all[rows]
        a1, a2, a3 = _split3(a_all[rows])
        a_cs = _dot(tril_bf, a1) + _dot(tril_bf, a2) + _dot(tril_bf, a3)
        a_last = a_cs[L - 1:L, :]
        a_cs_t = a_cs.T
        dt_t = dt.T
        w_col = jnp.exp(a_last - a_cs) * dt
        e_col = jnp.exp(a_cs)
        c_dec = jnp.exp(a_last)

        for g in range(G):
            B_g = Bm[:, g * N:(g + 1) * N]
            C_g = Cm[:, g * N:(g + 1) * N].astype(_BF)
            cb = _dot_nt(C_g, B_g.astype(_BF))
            b_t = B_g.T.astype(_BF)
            ys = []
            for pr in range(HG // 2):
                h0 = g * HG + 2 * pr
                h1 = h0 + 1
                cols = slice(h0 * P, (h0 + 2) * P)
                xs2 = xs[:, cols]
                pick = lambda v: jnp.where(lo, v[:, h0:h0 + 1], v[:, h1:h1 + 1])
                prev2 = state_ref[:, cols]
                y_off = _dot(C_g, prev2.astype(_BF)) * pick(e_col)
                ms = []
                for hh in (h0, h1):
                    seg = a_cs[:, hh:hh + 1] - a_cs_t[hh:hh + 1, :]
                    lmat = jnp.exp(jnp.where(tril, seg, -jnp.inf))
                    ms.append((cb * lmat * dt_t[hh:hh + 1, :]).astype(_BF))
                xs_lo = jnp.where(lo, xs2, 0.0).astype(_BF)
                xs_hi = jnp.where(lo, 0.0, xs2).astype(_BF)
                y_diag = _dot(jnp.concatenate(ms, axis=1), jnp.concatenate([xs_lo, xs_hi], axis=0))
                new_state = _dot(b_t, (xs2 * pick(w_col)).astype(_BF))
                state_ref[:, cols] = prev2 * pick(c_dec) + new_state
                y2 = y_diag + y_off + xs2 * dsk_ref[:, cols]
                ys.append(y2 * _silu(z_ref[rows, cols]))
            yg = jnp.concatenate(ys, axis=1)
            y_ref[rows, g * GW:(g + 1) * GW] = _rms(yg, nw_ref[:, g * GW:(g + 1) * GW]).astype(_BF)


def _ssd(z, xbc, dt_raw, cw, cb, dtb, alog, dsk, nw, *, B, S):
    T, DI = z.shape
    C = xbc.shape[1]
    L = SSD_CHUNK
    H, G, N = SSD_HEADS, SSD_GROUPS, SSD_STATE
    P = DI // H
    assert 2 * P == LANES and H % (2 * G) == 0 and S % L == 0
    nsub = math.gcd(SSD_CHUNKS_PER_STEP, S // L)
    nc = S // (L * nsub)
    row = lambda n: pl.BlockSpec((nsub * L, n), lambda b, c: (b * nc + c, 0))
    consts = (cw, cb, dtb, alog, dsk, nw)
    return pl.pallas_call(
        functools.partial(_ssd_kernel, L=L, H=H, P=P, G=G, N=N, nsub=nsub),
        grid=(B, nc),
        in_specs=[row(DI), row(C), row(LANES)] + [_const_spec(a.shape) for a in consts],
        out_specs=row(DI),
        out_shape=jax.ShapeDtypeStruct((T, DI), _BF),
        scratch_shapes=[pltpu.VMEM((SUBLANES, C), _F32), pltpu.VMEM((N, DI), _F32)],
        compiler_params=_params("arbitrary", "arbitrary"),
        name="ssd",
    )(z, xbc, dt_raw, *consts)


def _sb_kernel(q_ref, k_ref, v_ref, nw_ref, o_ref, acc_ref, r_ref, z_ref, p_ref, *, tq, hd, npp, nqb):
    lane = lax.broadcasted_iota(jnp.int32, (1, 2 * hd), 1)
    lo = lane < hd
    nh = 2 * npp
    ri = lax.broadcasted_iota(jnp.int32, (tq, tq), 0)
    ci = lax.broadcasted_iota(jnp.int32, (tq, tq), 1)
    tri = jnp.where(ri >= ci, 1.0, 0.0).astype(_BF)
    tri2 = jnp.concatenate([tri, tri], axis=0)
    rs = lax.broadcasted_iota(jnp.int32, (nh * tq, tq), 0) & (tq - 1)
    cs = lax.broadcasted_iota(jnp.int32, (nh * tq, tq), 1)
    causal = cs < rs
    first_tile = pl.program_id(2) * nqb

    def key_start(j):
        return pl.multiple_of(jnp.maximum(j, 0) * tq, tq)

    def weights(mode):
        nr = SB_TOP_ROWS if mode == "top" else tq

        def head_rows(ref):
            if mode != "top":
                return ref[...]
            return jnp.concatenate([ref[h * tq:h * tq + nr, :] for h in range(nh)], axis=0)

        z = head_rows(z_ref)
        if mode == "diag":
            z = jnp.where(causal, z, -jnp.inf)
        lnb = -_softplus(z)
        r = head_rows(r_ref)
        one = jnp.int32(1)
        if mode == "diag":
            live_top, live_rest = one, one
        else:
            tot = r[:, 0:1] + jnp.sum(lnb, axis=-1, keepdims=True)
            if mode == "top":
                live_top, live_rest = (jnp.max(tot) > SB_SKIP_LOG).astype(jnp.int32), jnp.int32(0)
            else:
                pick = lambda lo_row, hi_row: jnp.concatenate(
                    [tot[h * tq + lo_row:h * tq + hi_row] for h in range(nh)], axis=0)
                live_top = (jnp.max(pick(0, SB_TOP_ROWS)) > SB_SKIP_LOG).astype(jnp.int32)
                live_rest = (jnp.max(pick(SB_TOP_ROWS, tq)) > SB_SKIP_LOG).astype(jnp.int32)
        hi = lnb.astype(_BF)
        lw = (lnb - hi.astype(_F32)).astype(_BF)
        cum = _dot(jnp.concatenate([hi, lw], axis=1), tri2)
        a = jnp.exp(z + cum + r)
        r_new = r + cum[:, 0:1]
        if mode == "top":
            for h in range(nh):
                r_ref[h * tq:h * tq + nr, :] = r_new[h * nr:(h + 1) * nr]
        else:
            r_ref[...] = r_new
        a = a.astype(_BF)
        for p in range(npp):
            p_ref[p, 0:nr, :] = jnp.concatenate(
                [a[2 * p * nr:(2 * p + 1) * nr], a[(2 * p + 1) * nr:(2 * p + 2) * nr]], axis=1)
            if mode == "top":
                p_ref[p, nr:tq, :] = jnp.zeros((tq - nr, 2 * tq), _BF)
        return live_top, live_rest

    def apply_weights(j):
        start = key_start(j)
        for p in range(npp):
            cols = slice(p * LANES, (p + 1) * LANES)
            v2 = v_ref[pl.ds(start, tq), cols]
            zv = jnp.zeros_like(v2)
            vv = jnp.concatenate([jnp.where(lo, v2, zv), jnp.where(lo, zv, v2)], axis=0)
            acc_ref[:, cols] += _dot(p_ref[p], vv)

    def finish(qb):
        rows = pl.ds(pl.multiple_of(qb * tq, tq), tq)
        for p in range(npp):
            cols = slice(p * LANES, (p + 1) * LANES)
            acc = acc_ref[:, cols]
            sq = acc * acc
            s_lo = jnp.sum(jnp.where(lo, sq, 0.0), axis=-1, keepdims=True)
            s_hi = jnp.sum(jnp.where(lo, 0.0, sq), axis=-1, keepdims=True)
            inv = lax.rsqrt(jnp.where(lo, s_lo, s_hi) * (1.0 / hd) + EPS)
            o_ref[rows, cols] = (acc * inv * nw_ref[:, cols]).astype(_BF)

    acc_ref[...] = jnp.zeros_like(acc_ref)
    p_ref[...] = jnp.zeros_like(p_ref)

    def query_tile(qb, pending):
        i = first_tile + qb
        rows = pl.ds(pl.multiple_of(qb * tq, tq), tq)
        qcat = []
        for p in range(npp):
            q2 = q_ref[rows, p * LANES:(p + 1) * LANES]
            zq = jnp.zeros_like(q2)
            qcat.append(jnp.concatenate([jnp.where(lo, q2, zq), jnp.where(lo, zq, q2)], axis=0))

        def scores(j):
            start = key_start(j)
            return jnp.concatenate(
                [_dot_nt(qcat[p], k_ref[pl.ds(start, tq), p * LANES:(p + 1) * LANES]) for p in range(npp)],
                axis=0)

        z_first = scores(i)
        z_next = scores(i - 1)
        apply_weights(pending)
        finish(jnp.maximum(qb - 1, 0))
        acc_ref[...] = jnp.zeros_like(acc_ref)
        r_ref[...] = jnp.zeros_like(r_ref)
        z_ref[...] = z_first
        live_top0, live_rest0 = weights("diag")
        z_ref[...] = z_next

        def cond(carry):
            j, live_top, live_rest = carry
            return jnp.logical_and(j >= 0, (live_top + live_rest) > 0)

        def body(carry):
            j, _, live_rest = carry

            def step(mode):
                def run():
                    apply_weights(j + 1)
                    z_next = scores(j - 1)
                    live = weights(mode)
                    z_ref[...] = z_next
                    return live
                return run

            live_top, live_rest = lax.cond(live_rest > 0, step("full"), step("top"))
            return j - 1, live_top, live_rest

        j_end, _, _ = lax.while_loop(cond, body, (i - 1, live_top0, live_rest0))
        return j_end + 1

    pending = lax.fori_loop(0, nqb, query_tile, jnp.int32(0))
    apply_weights(pending)
    finish(nqb - 1)


def _sb_attention(q, k, v, nw, *, B, S):
    T, W = q.shape
    hd = SB_HEAD_DIM
    tq = SB_TILE
    npp = SB_PAIRS_PER_STEP
    cw = npp * LANES
    assert 2 * hd == LANES and tq == LANES and S % tq == 0 and W % cw == 0
    nqb = math.gcd(SB_QTILES_PER_STEP, S // tq)
    nq = S // (tq * nqb)
    k3 = k.reshape(B, S, W)
    v3 = v.reshape(B, S, W)
    return pl.pallas_call(
        functools.partial(_sb_kernel, tq=tq, hd=hd, npp=npp, nqb=nqb),
        grid=(B, W // cw, nq),
        in_specs=[
            pl.BlockSpec((nqb * tq, cw), lambda b, p, i: (b * nq + i, p)),
            pl.BlockSpec((None, S, cw), lambda b, p, i: (b, 0, p), pipeline_mode=pl.Buffered(1)),
            pl.BlockSpec((None, S, cw), lambda b, p, i: (b, 0, p), pipeline_mode=pl.Buffered(1)),
            pl.BlockSpec((1, cw), lambda b, p, i: (0, p)),
        ],
        out_specs=pl.BlockSpec((nqb * tq, cw), lambda b, p, i: (b * nq + i, p)),
        out_shape=jax.ShapeDtypeStruct((T, W), _BF),
        scratch_shapes=[pltpu.VMEM((tq, cw), _F32), pltpu.VMEM((2 * npp * tq, tq), _F32),
                        pltpu.VMEM((2 * npp * tq, tq), _F32), pltpu.VMEM((npp, tq, 2 * tq), _BF)],
        compiler_params=_params("arbitrary", "arbitrary", "arbitrary"),
        name="sb_attn",
    )(q, k3, v3, nw)


def _memkv_kernel(m_ref, nw_ref, wk_ref, wv_ref, k_ref, v_ref):
    mb = _rms(m_ref[...], nw_ref[...]).astype(_BF)
    k_ref[...] = _dot(mb, wk_ref[...]).astype(_BF)
    v_ref[...] = _dot(mb, wv_ref[...]).astype(_BF)


def _mem_kv(mem2d, nw, wk, wv):
    R, D = mem2d.shape
    return pl.pallas_call(
        _memkv_kernel,
        grid=(1,),
        in_specs=[_const_spec(a.shape) for a in (mem2d, nw, wk, wv)],
        out_specs=[_const_spec((R, D))] * 2,
        out_shape=[jax.ShapeDtypeStruct((R, D), _BF)] * 2,
        compiler_params=_params("arbitrary"),
        name="mem_kv",
    )(mem2d, nw, wk, wv)


def _mixmem_kernel(x_ref, ya_ref, yb_ref, wo1_ref, wo2_ref, nw_ref, wq_ref, mk_ref, mv_ref, wo_ref,
                   o_ref, *, heads, q_scale):
    x1 = x_ref[...] + _dot(ya_ref[...], wo1_ref[...]) + _dot(yb_ref[...], wo2_ref[...])
    hb = _rms(x1, nw_ref[...]).astype(_BF)
    q = (_dot(hb, wq_ref[...]) * q_scale).astype(_BF)
    D = q.shape[1]
    hd = D // heads
    outs = []
    for h in range(heads):
        cols = slice(h * hd, (h + 1) * hd)
        sc = _dot_nt(q[:, cols], mk_ref[:, cols])
        sc = sc - jnp.max(sc, axis=-1, keepdims=True)
        e = jnp.exp(sc)
        p = e / jnp.sum(e, axis=-1, keepdims=True)
        outs.append(_dot(p.astype(_BF), mv_ref[:, cols]).astype(_BF))
    o = jnp.concatenate(outs, axis=1)
    o_ref[...] = x1 + _dot(o, wo_ref[...])


def _mix_mem(x2d, ya, yb, wo1, wo2, nw, wq, mk, mv, wo, *, B, S, tm):
    T, D = x2d.shape
    M = mk.shape[0] // B
    nt = S // tm
    row = pl.BlockSpec((tm, D), lambda b, i: (b * nt + i, 0))
    memspec = pl.BlockSpec((M, D), lambda b, i: (b, 0))
    hd = D // MEM_HEADS
    return pl.pallas_call(
        functools.partial(_mixmem_kernel, heads=MEM_HEADS, q_scale=1.0 / math.sqrt(hd)),
        grid=(B, nt),
        in_specs=[row, row, row, _const_spec(wo1.shape), _const_spec(wo2.shape), _const_spec(nw.shape),
                  _const_spec(wq.shape), memspec, memspec, _const_spec(wo.shape)],
        out_specs=row,
        out_shape=jax.ShapeDtypeStruct((T, D), _F32),
        compiler_params=_params("arbitrary", "arbitrary"),
        name="mix_mem",
    )(x2d, ya, yb, wo1, wo2, nw, wq, mk, mv, wo)


def _ffn_kernel(x_ref, halo_ref, nw_ref, wup_ref, cw_ref, cb_ref, wdn_ref, fw_ref,
                o_ref, acc_ref, h_ref, ua_ref, ub_ref, aa_ref, ab_ref, *, tm, fc, dff, final_norm):
    i = pl.program_id(1)
    x = x_ref[...]
    keep = (i > 0).astype(_F32)
    h_ref[...] = jnp.concatenate(
        [_rms(halo_ref[...], nw_ref[...]) * keep, _rms(x, nw_ref[...])], axis=0).astype(_BF)
    acc_ref[...] = jnp.zeros_like(acc_ref)

    ubuf = (ua_ref, ub_ref)
    abuf = (aa_ref, ab_ref)
    starts = list(range(0, dff, fc))
    widths = [min(fc, dff - s0) for s0 in starts]
    nchunk = len(starts)

    def cols(c, base):
        return slice(base + starts[c], base + starts[c] + widths[c])

    def up(c):
        w = widths[c]
        ubuf[c % 2][0, :, :w] = _dot(h_ref[...], wup_ref[:, cols(c, 0)])
        ubuf[c % 2][1, :, :w] = _dot(h_ref[...], wup_ref[:, cols(c, dff)])

    def conv(u, cw, cb):
        out = cb + cw[FFN_CONV - 1:FFN_CONV, :] * u
        for kk in range(1, FFN_CONV):
            out = out + cw[FFN_CONV - 1 - kk:FFN_CONV - kk, :] * pltpu.roll(u, kk, 0)
        return out[FFN_HALO:]

    def gate(c):
        w = widths[c]
        g = conv(ubuf[c % 2][0, :, :w], cw_ref[:, cols(c, 0)], cb_ref[:, cols(c, 0)])
        val = conv(ubuf[c % 2][1, :, :w], cw_ref[:, cols(c, dff)], cb_ref[:, cols(c, dff)])
        abuf[c % 2][:, :w] = (_silu(g) * val).astype(_BF)

    def down(c):
        acc_ref[...] += _dot(abuf[c % 2][:, :widths[c]], wdn_ref[cols(c, 0), :])

    up(0)
    if nchunk > 1:
        up(1)
    gate(0)
    for c in range(nchunk):
        down(c)
        if c + 1 < nchunk:
            gate(c + 1)
        if c + 2 < nchunk:
            up(c + 2)
    y = x + acc_ref[...]
    o_ref[...] = _rms(y, fw_ref[...]) if final_norm else y


def _ffn(x2d, nw, wup, cw, cb, wdn, fw, *, B, S, tm, final_norm):
    T, D = x2d.shape
    dff = wdn.shape[0]
    fc = FFN_CHUNK
    assert dff % LANES == 0 and tm % FFN_HALO == 0
    nt = S // tm
    hb = tm // FFN_HALO
    row = pl.BlockSpec((tm, D), lambda b, i: (b * nt + i, 0))
    halo = pl.BlockSpec((FFN_HALO, D), lambda b, i: (jnp.maximum((b * nt + i) * hb - 1, 0), 0))
    return pl.pallas_call(
        functools.partial(_ffn_kernel, tm=tm, fc=fc, dff=dff, final_norm=final_norm),
        grid=(B, nt),
        in_specs=[row, halo] + [_const_spec(a.shape) for a in (nw, wup, cw, cb, wdn, fw)],
        out_specs=row,
        out_shape=jax.ShapeDtypeStruct((T, D), _F32),
        scratch_shapes=[pltpu.VMEM((tm, D), _F32),
                        pltpu.VMEM((tm + FFN_HALO, D), _BF),
                        pltpu.VMEM((2, tm + FFN_HALO, fc), _F32),
                        pltpu.VMEM((2, tm + FFN_HALO, fc), _F32),
                        pltpu.VMEM((tm, fc), _BF),
                        pltpu.VMEM((tm, fc), _BF)],
        compiler_params=_params("arbitrary", "arbitrary"),
        name="ffn",
    )(x2d, x2d, nw, wup, cw, cb, wdn, fw)


def _pad_lanes(a):
    return jnp.pad(a, ((0, 0), (0, LANES - a.shape[1])))


def kernel(x, mem, norm_mix_w, w_in, conv_ssd_w, conv_ssd_b, dt_bias, a_log, d_skip, ssd_norm_w, sb_norm_w, w_out, norm_mem_w, norm_memkv_w, w_mq, w_mk, w_mv, w_mo, norm_ffn_w, w_up, conv_ffn_w, conv_ffn_b, w_down, norm_final_w):
    B, S, D = x.shape
    T = B * S
    depth = w_in.shape[0]
    H = SSD_HEADS
    DI = ssd_norm_w.shape[1]
    C = conv_ssd_w.shape[2]
    W = sb_norm_w.shape[1]
    o1, o2 = DI, DI + C
    o3 = o2 + H
    o4, o5 = o3 + W, o3 + 2 * W
    tm = min(512, S)

    x2d = x.reshape(T, D)
    mem2d = mem.reshape(B * mem.shape[1], D)
    for l in range(depth):
        bf = lambda a: a.astype(_BF)
        z, xbc, dt_raw, q, k, v = _in_proj(x2d, norm_mix_w[l][None], bf(w_in[l]), (o1, o2, o3, o4, o5, o5 + W),
                                           tm=min(512, S))
        y_ssd = _ssd(z, xbc, dt_raw, conv_ssd_w[l], conv_ssd_b[l][None], _pad_lanes(dt_bias[l][None]),
                     _pad_lanes(a_log[l][None]), jnp.repeat(d_skip[l], DI // H)[None], ssd_norm_w[l][None],
                     B=B, S=S)
        y_sb = _sb_attention(q, k, v, sb_norm_w[l][None], B=B, S=S)
        mk, mv = _mem_kv(mem2d, norm_memkv_w[l][None], bf(w_mk[l]), bf(w_mv[l]))
        x2d = _mix_mem(x2d, y_ssd, y_sb, bf(w_out[l][:DI]), bf(w_out[l][DI:]), norm_mem_w[l][None],
                       bf(w_mq[l]), mk, mv, bf(w_mo[l]), B=B, S=S, tm=min(1024, S))
        x2d = _ffn(x2d, norm_ffn_w[l][None], bf(w_up[l]), conv_ffn_w[l], conv_ffn_b[l][None], bf(w_down[l]),
                   norm_final_w[None], B=B, S=S, tm=tm, final_norm=(l == depth - 1))
    return x2d.reshape(B, S, D)
```

```python
import functools
import math

import jax
import jax.numpy as jnp
from jax import lax
from jax.experimental import pallas as pl
from jax.experimental.pallas import tpu as pltpu

EPS = 1e-6
LANES = 128
SUBLANES = 8
VMEM_LIMIT = 56 * 1024 * 1024

SSD_HEADS = 16
SSD_GROUPS = 2
SSD_STATE = 128
SSD_CONV = 4
SSD_CHUNK = 128
SSD_CHUNKS_PER_STEP = 8
SB_HEAD_DIM = 64
SB_TILE = 128
SB_PAIRS_PER_STEP = 8
SB_TOP_ROWS = 48
SB_QTILES_PER_STEP = 8
MEM_HEADS = 4
IN_PROJ_ROWS = 512
MIX_ROWS = 1024
FFN_ROWS = 512
FFN_CONV = 3
FFN_CHUNK = 768
FFN_HALO = 16

SB_SKIP_LOG = -110.0

_BF = jnp.bfloat16
_F32 = jnp.float32


def _dot(a, b):
    return jnp.dot(a, b, preferred_element_type=_F32)


def _dot_nt(a, b):
    return lax.dot_general(a, b, (((1,), (1,)), ((), ())), preferred_element_type=_F32)


def _rms(x, w):
    return x * lax.rsqrt(jnp.mean(x * x, axis=-1, keepdims=True) + EPS) * w


def _softplus(x):
    return jnp.maximum(x, 0.0) + jnp.log(1.0 + jnp.exp(-jnp.abs(x)))


def _silu(x):
    return x * jax.nn.sigmoid(x)


def _params(*sem):
    return pltpu.CompilerParams(dimension_semantics=sem, vmem_limit_bytes=VMEM_LIMIT)


def _const_spec(shape):
    nd = len(shape)
    return pl.BlockSpec(shape, lambda *_: (0,) * nd, pipeline_mode=pl.Buffered(1))


def _inproj_kernel(x_ref, nw_ref, w_ref, z_ref, xbc_ref, dt_ref, q_ref, k_ref, v_ref,
                   wq_ref, wk_ref, wv_ref, *, q_scale, offs):
    o1, o2, o3, o4, o5, o6 = offs

    @pl.when(pl.program_id(0) == 0)
    def _():
        wq_ref[...] = w_ref[:, o3:o4]
        wk_ref[...] = w_ref[:, o4:o5]
        wv_ref[...] = w_ref[:, o5:o6]

    hb = _rms(x_ref[...], nw_ref[...]).astype(_BF)
    z_ref[...] = _dot(hb, w_ref[:, 0:o1])
    xbc_ref[...] = _dot(hb, w_ref[:, o1:o2])
    dt_ref[...] = _dot(hb, w_ref[:, o2:o2 + LANES])
    q_ref[...] = (_dot(hb, wq_ref[...]) * q_scale).astype(_BF)
    k_ref[...] = _dot(hb, wk_ref[...]).astype(_BF)
    v_ref[...] = _dot(hb, wv_ref[...]).astype(_BF)


def _in_proj(x2d, nw, w, offs, *, tm):
    T, D = x2d.shape
    o1, o2, o3, o4, o5, o6 = offs
    assert o1 % LANES == 0 and o2 % LANES == 0 and o2 + LANES <= o6
    row = lambda n: pl.BlockSpec((tm, n), lambda i: (i, 0))
    widths = (o1, o2 - o1, LANES, o4 - o3, o5 - o4, o6 - o5)
    out_dtypes = (_F32, _F32, _F32, _BF, _BF, _BF)
    return pl.pallas_call(
        functools.partial(_inproj_kernel, q_scale=1.0 / math.sqrt(SB_HEAD_DIM), offs=offs),
        grid=(T // tm,),
        in_specs=[row(D), _const_spec(nw.shape), _const_spec(w.shape)],
        out_specs=[row(n) for n in widths],
        out_shape=[jax.ShapeDtypeStruct((T, n), dt) for n, dt in zip(widths, out_dtypes)],
        scratch_shapes=[pltpu.VMEM((D, n), _BF) for n in widths[3:]],
        compiler_params=_params("arbitrary"),
        name="in_proj",
    )(x2d, nw, w)


def _split3(v):
    v1 = v.astype(_BF)
    r1 = v - v1.astype(_F32)
    v2 = r1.astype(_BF)
    v3 = (r1 - v2.astype(_F32)).astype(_BF)
    return v1, v2, v3


def _ssd_kernel(z_ref, xbc_ref, dt_ref, cw_ref, cb_ref, dtb_ref, alog_ref, dsk_ref, nw_ref,
                y_ref, tail_ref, state_ref, *, L, H, P, G, N, nsub):
    DI = H * P
    HG = H // G
    GW = DI // G
    R = nsub * L
    c = pl.program_id(1)

    @pl.when(c == 0)
    def _():
        tail_ref[...] = jnp.zeros_like(tail_ref)
        state_ref[...] = jnp.zeros_like(state_ref)

    xbc = xbc_ref[...]
    ext = jnp.concatenate([tail_ref[...], xbc], axis=0)
    tail_ref[...] = xbc[R - SUBLANES:, :]
    conv = cb_ref[...] + cw_ref[SSD_CONV - 1:SSD_CONV, :] * xbc
    for kk in range(1, SSD_CONV):
        conv = conv + cw_ref[SSD_CONV - 1 - kk:SSD_CONV - kk, :] * pltpu.roll(ext, kk, 0)[SUBLANES:]
    act_all = _silu(conv)
    dt_all = _softplus(dt_ref[...] + dtb_ref[...])
    a_all = dt_all * (-jnp.exp(alog_ref[...]))

    ri = lax.broadcasted_iota(jnp.int32, (L, L), 0)
    ci = lax.broadcasted_iota(jnp.int32, (L, L), 1)
    tril = ri >= ci
    tril_bf = jnp.where(tril, 1.0, 0.0).astype(_BF)
    lane = lax.broadcasted_iota(jnp.int32, (1, 2 * P), 1)
    lo = lane < P

    for sc in range(nsub):
        rows = slice(sc * L, (sc + 1) * L)
        act = act_all[rows]
        xs = act[:, :DI]
        Bm = act[:, DI:DI + G * N]
        Cm = act[:, DI + G * N:]
        dt = dt_all[rows]
        a1, a2, a3 = _split3(a_all[rows])
        a_cs = _dot(tril_bf, a1) + _dot(tril_bf, a2) + _dot(tril_bf, a3)
        a_last = a_cs[L - 1:L, :]
        a_cs_t = a_cs.T
        dt_t = dt.T
        w_col = jnp.exp(a_last - a_cs) * dt
        e_col = jnp.exp(a_cs)
        c_dec = jnp.exp(a_last)

        for g in range(G):
            B_g = Bm[:, g * N:(g + 1) * N]
            C_g = Cm[:, g * N:(g + 1) * N].astype(_BF)
            cb = _dot_nt(C_g, B_g.astype(_BF))
            b_t = B_g.T.astype(_BF)
            ys = []
            for pr in range(HG // 2):
                h0 = g * HG + 2 * pr
                h1 = h0 + 1
                cols = slice(h0 * P, (h0 + 2) * P)
                xs2 = xs[:, cols]
                pick = lambda v: jnp.where(lo, v[:, h0:h0 + 1], v[:, h1:h1 + 1])
                prev2 = state_ref[:, cols]
                y_off = _dot(C_g, prev2.astype(_BF)) * pick(e_col)
                ms = []
                for hh in (h0, h1):
                    seg = a_cs[:, hh:hh + 1] - a_cs_t[hh:hh + 1, :]
                    lmat = jnp.exp(jnp.where(tril, seg, -jnp.inf))
                    ms.append((cb * lmat * dt_t[hh:hh + 1, :]).astype(_BF))
                xs_lo = jnp.where(lo, xs2, 0.0).astype(_BF)
                xs_hi = jnp.where(lo, 0.0, xs2).astype(_BF)
                y_diag = _dot(jnp.concatenate(ms, axis=1), jnp.concatenate([xs_lo, xs_hi], axis=0))
                new_state = _dot(b_t, (xs2 * pick(w_col)).astype(_BF))
                state_ref[:, cols] = prev2 * pick(c_dec) + new_state
                y2 = y_diag + y_off + xs2 * dsk_ref[:, cols]
                ys.append(y2 * _silu(z_ref[rows, cols]))
            yg = jnp.concatenate(ys, axis=1)
            y_ref[rows, g * GW:(g + 1) * GW] = _rms(yg, nw_ref[:, g * GW:(g + 1) * GW]).astype(_BF)


def _ssd(z, xbc, dt_raw, cw, cb, dtb, alog, dsk, nw, *, B, S):
    T, DI = z.shape
    C = xbc.shape[1]
    L = SSD_CHUNK
    H, G, N = SSD_HEADS, SSD_GROUPS, SSD_STATE
    P = DI // H
    assert 2 * P == LANES and H % (2 * G) == 0 and S % L == 0
    nsub = math.gcd(SSD_CHUNKS_PER_STEP, S // L)
    nc = S // (L * nsub)
    row = lambda n: pl.BlockSpec((nsub * L, n), lambda b, c: (b * nc + c, 0))
    consts = (cw, cb, dtb, alog, dsk, nw)
    return pl.pallas_call(
        functools.partial(_ssd_kernel, L=L, H=H, P=P, G=G, N=N, nsub=nsub),
        grid=(B, nc),
        in_specs=[row(DI), row(C), row(LANES)] + [_const_spec(a.shape) for a in consts],
        out_specs=row(DI),
        out_shape=jax.ShapeDtypeStruct((T, DI), _BF),
        scratch_shapes=[pltpu.VMEM((SUBLANES, C), _F32), pltpu.VMEM((N, DI), _F32)],
        compiler_params=_params("arbitrary", "arbitrary"),
        name="ssd",
    )(z, xbc, dt_raw, *consts)


def _sb_kernel(q_ref, k_ref, v_ref, nw_ref, o_ref, acc_ref, r_ref, z_ref, p_ref, *, tq, hd, npp, nqb):
    lane = lax.broadcasted_iota(jnp.int32, (1, 2 * hd), 1)
    lo = lane < hd
    nh = 2 * npp
    ri = lax.broadcasted_iota(jnp.int32, (tq, tq), 0)
    ci = lax.broadcasted_iota(jnp.int32, (tq, tq), 1)
    tri = jnp.where(ri >= ci, 1.0, 0.0).astype(_BF)
    tri2 = jnp.concatenate([tri, tri], axis=0)
    rs = lax.broadcasted_iota(jnp.int32, (nh * tq, tq), 0) & (tq - 1)
    cs = lax.broadcasted_iota(jnp.int32, (nh * tq, tq), 1)
    causal = cs < rs
    first_tile = pl.program_id(2) * nqb

    def key_start(j):
        return pl.multiple_of(jnp.maximum(j, 0) * tq, tq)

    def weights(mode):
        nr = SB_TOP_ROWS if mode == "top" else tq

        def head_rows(ref):
            if mode != "top":
                return ref[...]
            return jnp.concatenate([ref[h * tq:h * tq + nr, :] for h in range(nh)], axis=0)

        z = head_rows(z_ref)
        if mode == "diag":
            z = jnp.where(causal, z, -jnp.inf)
        lnb = -_softplus(z)
        r = head_rows(r_ref)
        one = jnp.int32(1)
        if mode == "diag":
            live_top, live_rest = one, one
        else:
            tot = r[:, 0:1] + jnp.sum(lnb, axis=-1, keepdims=True)
            if mode == "top":
                live_top, live_rest = (jnp.max(tot) > SB_SKIP_LOG).astype(jnp.int32), jnp.int32(0)
            else:
                pick = lambda lo_row, hi_row: jnp.concatenate(
                    [tot[h * tq + lo_row:h * tq + hi_row] for h in range(nh)], axis=0)
                live_top = (jnp.max(pick(0, SB_TOP_ROWS)) > SB_SKIP_LOG).astype(jnp.int32)
                live_rest = (jnp.max(pick(SB_TOP_ROWS, tq)) > SB_SKIP_LOG).astype(jnp.int32)
        hi = lnb.astype(_BF)
        lw = (lnb - hi.astype(_F32)).astype(_BF)
        cum = _dot(jnp.concatenate([hi, lw], axis=1), tri2)
        a = jnp.exp(z + cum + r)
        r_new = r + cum[:, 0:1]
        if mode == "top":
            for h in range(nh):
                r_ref[h * tq:h * tq + nr, :] = r_new[h * nr:(h + 1) * nr]
        else:
            r_ref[...] = r_new
        a = a.astype(_BF)
        for p in range(npp):
            p_ref[p, 0:nr, :] = jnp.concatenate(
                [a[2 * p * nr:(2 * p + 1) * nr], a[(2 * p + 1) * nr:(2 * p + 2) * nr]], axis=1)
            if mode == "top":
                p_ref[p, nr:tq, :] = jnp.zeros((tq - nr, 2 * tq), _BF)
        return live_top, live_rest

    def apply_weights(j):
        start = key_start(j)
        for p in range(npp):
            cols = slice(p * LANES, (p + 1) * LANES)
            v2 = v_ref[pl.ds(start, tq), cols]
            zv = jnp.zeros_like(v2)
            vv = jnp.concatenate([jnp.where(lo, v2, zv), jnp.where(lo, zv, v2)], axis=0)
            acc_ref[:, cols] += _dot(p_ref[p], vv)

    def finish(qb):
        rows = pl.ds(pl.multiple_of(qb * tq, tq), tq)
        for p in range(npp):
            cols = slice(p * LANES, (p + 1) * LANES)
            acc = acc_ref[:, cols]
            sq = acc * acc
            s_lo = jnp.sum(jnp.where(lo, sq, 0.0), axis=-1, keepdims=True)
            s_hi = jnp.sum(jnp.where(lo, 0.0, sq), axis=-1, keepdims=True)
            inv = lax.rsqrt(jnp.where(lo, s_lo, s_hi) * (1.0 / hd) + EPS)
            o_ref[rows, cols] = (acc * inv * nw_ref[:, cols]).astype(_BF)

    acc_ref[...] = jnp.zeros_like(acc_ref)
    p_ref[...] = jnp.zeros_like(p_ref)

    def query_tile(qb, pending):
        i = first_tile + qb
        rows = pl.ds(pl.multiple_of(qb * tq, tq), tq)
        qcat = []
        for p in range(npp):
            q2 = q_ref[rows, p * LANES:(p + 1) * LANES]
            zq = jnp.zeros_like(q2)
            qcat.append(jnp.concatenate([jnp.where(lo, q2, zq), jnp.where(lo, zq, q2)], axis=0))

        def scores(j):
            start = key_start(j)
            return jnp.concatenate(
                [_dot_nt(qcat[p], k_ref[pl.ds(start, tq), p * LANES:(p + 1) * LANES]) for p in range(npp)],
                axis=0)

        z_first = scores(i)
        z_next = scores(i - 1)
        apply_weights(pending)
        finish(jnp.maximum(qb - 1, 0))
        acc_ref[...] = jnp.zeros_like(acc_ref)
        r_ref[...] = jnp.zeros_like(r_ref)
        z_ref[...] = z_first
        live_top0, live_rest0 = weights("diag")
        z_ref[...] = z_next

        def cond(carry):
            j, live_top, live_rest = carry
            return jnp.logical_and(j >= 0, (live_top + live_rest) > 0)

        def body(carry):
            j, _, live_rest = carry

            def step(mode):
                def run():
                    apply_weights(j + 1)
                    z_next = scores(j - 1)
                    live = weights(mode)
                    z_ref[...] = z_next
                    return live
                return run

            live_top, live_rest = lax.cond(live_rest > 0, step("full"), step("top"))
            return j - 1, live_top, live_rest

        j_end, _, _ = lax.while_loop(cond, body, (i - 1, live_top0, live_rest0))
        return j_end + 1

    pending = lax.fori_loop(0, nqb, query_tile, jnp.int32(0))
    apply_weights(pending)
    finish(nqb - 1)


def _sb_attention(q, k, v, nw, *, B, S):
    T, W = q.shape
    hd = SB_HEAD_DIM
    tq = SB_TILE
    npp = SB_PAIRS_PER_STEP
    cw = npp * LANES
    assert 2 * hd == LANES and tq == LANES and S % tq == 0 and W % cw == 0
    nqb = math.gcd(SB_QTILES_PER_STEP, S // tq)
    nq = S // (tq * nqb)
    k3 = k.reshape(B, S, W)
    v3 = v.reshape(B, S, W)
    return pl.pallas_call(
        functools.partial(_sb_kernel, tq=tq, hd=hd, npp=npp, nqb=nqb),
        grid=(B, W // cw, nq),
        in_specs=[
            pl.BlockSpec((nqb * tq, cw), lambda b, p, i: (b * nq + i, p)),
            pl.BlockSpec((None, S, cw), lambda b, p, i: (b, 0, p), pipeline_mode=pl.Buffered(1)),
            pl.BlockSpec((None, S, cw), lambda b, p, i: (b, 0, p), pipeline_mode=pl.Buffered(1)),
            pl.BlockSpec((1, cw), lambda b, p, i: (0, p)),
        ],
        out_specs=pl.BlockSpec((nqb * tq, cw), lambda b, p, i: (b * nq + i, p)),
        out_shape=jax.ShapeDtypeStruct((T, W), _BF),
        scratch_shapes=[pltpu.VMEM((tq, cw), _F32), pltpu.VMEM((2 * npp * tq, tq), _F32),
                        pltpu.VMEM((2 * npp * tq, tq), _F32), pltpu.VMEM((npp, tq, 2 * tq), _BF)],
        compiler_params=_params("arbitrary", "arbitrary", "arbitrary"),
        name="sb_attn",
    )(q, k3, v3, nw)


def _memkv_kernel(m_ref, nw_ref, wk_ref, wv_ref, k_ref, v_ref):
    mb = _rms(m_ref[...], nw_ref[...]).astype(_BF)
    k_ref[...] = _dot(mb, wk_ref[...]).astype(_BF)
    v_ref[...] = _dot(mb, wv_ref[...]).astype(_BF)


def _mem_kv(mem2d, nw, wk, wv):
    R, D = mem2d.shape
    return pl.pallas_call(
        _memkv_kernel,
        grid=(1,),
        in_specs=[_const_spec(a.shape) for a in (mem2d, nw, wk, wv)],
        out_specs=[_const_spec((R, D))] * 2,
        out_shape=[jax.ShapeDtypeStruct((R, D), _BF)] * 2,
        compiler_params=_params("arbitrary"),
        name="mem_kv",
    )(mem2d, nw, wk, wv)


def _mixmem_kernel(x_ref, ya_ref, yb_ref, wo1_ref, wo2_ref, nw_ref, wq_ref, mk_ref, mv_ref, wo_ref,
                   o_ref, *, heads, q_scale):
    x1 = x_ref[...] + _dot(ya_ref[...], wo1_ref[...]) + _dot(yb_ref[...], wo2_ref[...])
    hb = _rms(x1, nw_ref[...]).astype(_BF)
    q = (_dot(hb, wq_ref[...]) * q_scale).astype(_BF)
    D = q.shape[1]
    hd = D // heads
    outs = []
    for h in range(heads):
        cols = slice(h * hd, (h + 1) * hd)
        sc = _dot_nt(q[:, cols], mk_ref[:, cols])
        sc = sc - jnp.max(sc, axis=-1, keepdims=True)
        e = jnp.exp(sc)
        p = e / jnp.sum(e, axis=-1, keepdims=True)
        outs.append(_dot(p.astype(_BF), mv_ref[:, cols]).astype(_BF))
    o = jnp.concatenate(outs, axis=1)
    o_ref[...] = x1 + _dot(o, wo_ref[...])


def _mix_mem(x2d, ya, yb, wo1, wo2, nw, wq, mk, mv, wo, *, B, S, tm):
    T, D = x2d.shape
    M = mk.shape[0] // B
    nt = S // tm
    row = pl.BlockSpec((tm, D), lambda b, i: (b * nt + i, 0))
    memspec = pl.BlockSpec((M, D), lambda b, i: (b, 0))
    hd = D // MEM_HEADS
    return pl.pallas_call(
        functools.partial(_mixmem_kernel, heads=MEM_HEADS, q_scale=1.0 / math.sqrt(hd)),
        grid=(B, nt),
        in_specs=[row, row, row, _const_spec(wo1.shape), _const_spec(wo2.shape), _const_spec(nw.shape),
                  _const_spec(wq.shape), memspec, memspec, _const_spec(wo.shape)],
        out_specs=row,
        out_shape=jax.ShapeDtypeStruct((T, D), _F32),
        compiler_params=_params("arbitrary", "arbitrary"),
        name="mix_mem",
    )(x2d, ya, yb, wo1, wo2, nw, wq, mk, mv, wo)


def _ffn_kernel(x_ref, halo_ref, nw_ref, wup_ref, cw_ref, cb_ref, wdn_ref, fw_ref,
                o_ref, acc_ref, h_ref, ua_ref, ub_ref, aa_ref, ab_ref, *, tm, fc, dff, final_norm):
    i = pl.program_id(1)
    x = x_ref[...]
    keep = (i > 0).astype(_F32)
    h_ref[...] = jnp.concatenate(
        [_rms(halo_ref[...], nw_ref[...]) * keep, _rms(x, nw_ref[...])], axis=0).astype(_BF)
    acc_ref[...] = jnp.zeros_like(acc_ref)

    ubuf = (ua_ref, ub_ref)
    abuf = (aa_ref, ab_ref)
    starts = list(range(0, dff, fc))
    widths = [min(fc, dff - s0) for s0 in starts]
    nchunk = len(starts)

    def cols(c, base):
        return slice(base + starts[c], base + starts[c] + widths[c])

    def up(c):
        w = widths[c]
        ubuf[c % 2][0, :, :w] = _dot(h_ref[...], wup_ref[:, cols(c, 0)])
        ubuf[c % 2][1, :, :w] = _dot(h_ref[...], wup_ref[:, cols(c, dff)])

    def conv(u, cw, cb):
        out = cb + cw[FFN_CONV - 1:FFN_CONV, :] * u
        for kk in range(1, FFN_CONV):
            out = out + cw[FFN_CONV - 1 - kk:FFN_CONV - kk, :] * pltpu.roll(u, kk, 0)
        return out[FFN_HALO:]

    def gate(c):
        w = widths[c]
        g = conv(ubuf[c % 2][0, :, :w], cw_ref[:, cols(c, 0)], cb_ref[:, cols(c, 0)])
        val = conv(ubuf[c % 2][1, :, :w], cw_ref[:, cols(c, dff)], cb_ref[:, cols(c, dff)])
        abuf[c % 2][:, :w] = (_silu(g) * val).astype(_BF)

    def down(c):
        acc_ref[...] += _dot(abuf[c % 2][:, :widths[c]], wdn_ref[cols(c, 0), :])

    up(0)
    if nchunk > 1:
        up(1)
    gate(0)
    for c in range(nchunk):
        down(c)
        if c + 1 < nchunk:
            gate(c + 1)
        if c + 2 < nchunk:
            up(c + 2)
    y = x + acc_ref[...]
    o_ref[...] = _rms(y, fw_ref[...]) if final_norm else y


def _ffn(x2d, nw, wup, cw, cb, wdn, fw, *, B, S, tm, final_norm):
    T, D = x2d.shape
    dff = wdn.shape[0]
    fc = FFN_CHUNK
    assert dff % LANES == 0 and tm % FFN_HALO == 0
    nt = S // tm
    hb = tm // FFN_HALO
    row = pl.BlockSpec((tm, D), lambda b, i: (b * nt + i, 0))
    halo = pl.BlockSpec((FFN_HALO, D), lambda b, i: (jnp.maximum((b * nt + i) * hb - 1, 0), 0))
    return pl.pallas_call(
        functools.partial(_ffn_kernel, tm=tm, fc=fc, dff=dff, final_norm=final_norm),
        grid=(B, nt),
        in_specs=[row, halo] + [_const_spec(a.shape) for a in (nw, wup, cw, cb, wdn, fw)],
        out_specs=row,
        out_shape=jax.ShapeDtypeStruct((T, D), _F32),
        scratch_shapes=[pltpu.VMEM((tm, D), _F32),
                        pltpu.VMEM((tm + FFN_HALO, D), _BF),
                        pltpu.VMEM((2, tm + FFN_HALO, fc), _F32),
                        pltpu.VMEM((2, tm + FFN_HALO, fc), _F32),
                        pltpu.VMEM((tm, fc), _BF),
                        pltpu.VMEM((tm, fc), _BF)],
        compiler_params=_params("arbitrary", "arbitrary"),
        name="ffn",
    )(x2d, x2d, nw, wup, cw, cb, wdn, fw)


def _pad_lanes(a):
    return jnp.pad(a, ((0, 0), (0, LANES - a.shape[1])))


def kernel(x, mem, norm_mix_w, w_in, conv_ssd_w, conv_ssd_b, dt_bias, a_log, d_skip, ssd_norm_w, sb_norm_w, w_out, norm_mem_w, norm_memkv_w, w_mq, w_mk, w_mv, w_mo, norm_ffn_w, w_up, conv_ffn_w, conv_ffn_b, w_down, norm_final_w):
    B, S, D = x.shape
    T = B * S
    depth = w_in.shape[0]
    H = SSD_HEADS
    DI = ssd_norm_w.shape[1]
    C = conv_ssd_w.shape[2]
    W = sb_norm_w.shape[1]
    o1, o2 = DI, DI + C
    o3 = o2 + H
    o4, o5 = o3 + W, o3 + 2 * W

    x2d = x.reshape(T, D)
    mem2d = mem.reshape(B * mem.shape[1], D)
    for l in range(depth):
        bf = lambda a: a.astype(_BF)
        z, xbc, dt_raw, q, k, v = _in_proj(x2d, norm_mix_w[l][None], bf(w_in[l]), (o1, o2, o3, o4, o5, o5 + W),
                                           tm=min(IN_PROJ_ROWS, S))
        y_ssd = _ssd(z, xbc, dt_raw, conv_ssd_w[l], conv_ssd_b[l][None], _pad_lanes(dt_bias[l][None]),
                     _pad_lanes(a_log[l][None]), jnp.repeat(d_skip[l], DI // H)[None], ssd_norm_w[l][None],
                     B=B, S=S)
        y_sb = _sb_attention(q, k, v, sb_norm_w[l][None], B=B, S=S)
        mk, mv = _mem_kv(mem2d, norm_memkv_w[l][None], bf(w_mk[l]), bf(w_mv[l]))
        x2d = _mix_mem(x2d, y_ssd, y_sb, bf(w_out[l][:DI]), bf(w_out[l][DI:]), norm_mem_w[l][None],
                       bf(w_mq[l]), mk, mv, bf(w_mo[l]), B=B, S=S, tm=min(MIX_ROWS, S))
        x2d = _ffn(x2d, norm_ffn_w[l][None], bf(w_up[l]), conv_ffn_w[l], conv_ffn_b[l][None], bf(w_down[l]),
                   norm_final_w[None], B=B, S=S, tm=min(FFN_ROWS, S), final_norm=(l == depth - 1))
    return x2d.reshape(B, S, D)
```

```python
import functools
import math

import jax
import jax.numpy as jnp
from jax import lax
from jax.experimental import pallas as pl
from jax.experimental.pallas import tpu as pltpu

EPS = 1e-6
LANES = 128
SUBLANES = 8
VMEM_LIMIT = 56 * 1024 * 1024

SSD_HEADS = 16
SSD_GROUPS = 2
SSD_STATE = 128
SSD_CONV = 4
SSD_CHUNK = 128
SSD_CHUNKS_PER_STEP = 4
SB_HEAD_DIM = 64
SB_TILE = 128
SB_PAIRS_PER_STEP = 8
SB_TOP_ROWS = 48
SB_QTILES_PER_STEP = 8
MEM_HEADS = 4
IN_PROJ_ROWS = 512
MIX_ROWS = 1024
FFN_ROWS = 512
FFN_CONV = 3
FFN_CHUNK = 768
FFN_HALO = 16

SB_SKIP_LOG = -110.0

_BF = jnp.bfloat16
_F32 = jnp.float32


def _dot(a, b):
    return jnp.dot(a, b, preferred_element_type=_F32)


def _dot_nt(a, b):
    return lax.dot_general(a, b, (((1,), (1,)), ((), ())), preferred_element_type=_F32)


def _rms(x, w):
    return x * lax.rsqrt(jnp.mean(x * x, axis=-1, keepdims=True) + EPS) * w


def _softplus(x):
    return jnp.maximum(x, 0.0) + jnp.log(1.0 + jnp.exp(-jnp.abs(x)))


def _silu(x):
    return x * jax.nn.sigmoid(x)


def _params(*sem):
    return pltpu.CompilerParams(dimension_semantics=sem, vmem_limit_bytes=VMEM_LIMIT)


def _const_spec(shape):
    nd = len(shape)
    return pl.BlockSpec(shape, lambda *_: (0,) * nd, pipeline_mode=pl.Buffered(1))


def _inproj_kernel(x_ref, nw_ref, w_ref, z_ref, xbc_ref, dt_ref, q_ref, k_ref, v_ref,
                   wq_ref, wk_ref, wv_ref, *, q_scale, offs):
    o1, o2, o3, o4, o5, o6 = offs

    @pl.when(pl.program_id(0) == 0)
    def _():
        wq_ref[...] = w_ref[:, o3:o4]
        wk_ref[...] = w_ref[:, o4:o5]
        wv_ref[...] = w_ref[:, o5:o6]

    hb = _rms(x_ref[...], nw_ref[...]).astype(_BF)
    z_ref[...] = _dot(hb, w_ref[:, 0:o1])
    xbc_ref[...] = _dot(hb, w_ref[:, o1:o2])
    dt_ref[...] = _dot(hb, w_ref[:, o2:o2 + LANES])
    q_ref[...] = (_dot(hb, wq_ref[...]) * q_scale).astype(_BF)
    k_ref[...] = _dot(hb, wk_ref[...]).astype(_BF)
    v_ref[...] = _dot(hb, wv_ref[...]).astype(_BF)


def _in_proj(x2d, nw, w, offs, *, tm):
    T, D = x2d.shape
    o1, o2, o3, o4, o5, o6 = offs
    assert o1 % LANES == 0 and o2 % LANES == 0 and o2 + LANES <= o6
    row = lambda n: pl.BlockSpec((tm, n), lambda i: (i, 0))
    widths = (o1, o2 - o1, LANES, o4 - o3, o5 - o4, o6 - o5)
    out_dtypes = (_F32, _F32, _F32, _BF, _BF, _BF)
    return pl.pallas_call(
        functools.partial(_inproj_kernel, q_scale=1.0 / math.sqrt(SB_HEAD_DIM), offs=offs),
        grid=(T // tm,),
        in_specs=[row(D), _const_spec(nw.shape), _const_spec(w.shape)],
        out_specs=[row(n) for n in widths],
        out_shape=[jax.ShapeDtypeStruct((T, n), dt) for n, dt in zip(widths, out_dtypes)],
        scratch_shapes=[pltpu.VMEM((D, n), _BF) for n in widths[3:]],
        compiler_params=_params("arbitrary"),
        name="in_proj",
    )(x2d, nw, w)


def _split3(v):
    v1 = v.astype(_BF)
    r1 = v - v1.astype(_F32)
    v2 = r1.astype(_BF)
    v3 = (r1 - v2.astype(_F32)).astype(_BF)
    return v1, v2, v3


def _ssd_kernel(z_ref, xbc_ref, dt_ref, cw_ref, cb_ref, dtb_ref, alog_ref, dsk_ref, nw_ref,
                y_ref, tail_ref, state_ref, *, L, H, P, G, N, nsub):
    DI = H * P
    HG = H // G
    GW = DI // G
    R = nsub * L
    c = pl.program_id(1)

    @pl.when(c == 0)
    def _():
        tail_ref[...] = jnp.zeros_like(tail_ref)
        state_ref[...] = jnp.zeros_like(state_ref)

    xbc = xbc_ref[...]
    ext = jnp.concatenate([tail_ref[...], xbc], axis=0)
    tail_ref[...] = xbc[R - SUBLANES:, :]
    conv = cb_ref[...] + cw_ref[SSD_CONV - 1:SSD_CONV, :] * xbc
    for kk in range(1, SSD_CONV):
        conv = conv + cw_ref[SSD_CONV - 1 - kk:SSD_CONV - kk, :] * pltpu.roll(ext, kk, 0)[SUBLANES:]
    act_all = _silu(conv)
    dt_all = _softplus(dt_ref[...] + dtb_ref[...])
    a_all = dt_all * (-jnp.exp(alog_ref[...]))

    ri = lax.broadcasted_iota(jnp.int32, (L, L), 0)
    ci = lax.broadcasted_iota(jnp.int32, (L, L), 1)
    tril = ri >= ci
    tril_bf = jnp.where(tril, 1.0, 0.0).astype(_BF)
    lane = lax.broadcasted_iota(jnp.int32, (1, 2 * P), 1)
    lo = lane < P

    for sc in range(nsub):
        rows = slice(sc * L, (sc + 1) * L)
        act = act_all[rows]
        xs = act[:, :DI]
        Bm = act[:, DI:DI + G * N]
        Cm = act[:, DI + G * N:]
        dt = dt_all[rows]
        a1, a2, a3 = _split3(a_all[rows])
        a_cs = _dot(tril_bf, a1) + _dot(tril_bf, a2) + _dot(tril_bf, a3)
        a_last = a_cs[L - 1:L, :]
        a_cs_t = a_cs.T
        dt_t = dt.T
        w_col = jnp.exp(a_last - a_cs) * dt
        e_col = jnp.exp(a_cs)
        c_dec = jnp.exp(a_last)

        for g in range(G):
            B_g = Bm[:, g * N:(g + 1) * N]
            C_g = Cm[:, g * N:(g + 1) * N].astype(_BF)
            cb = _dot_nt(C_g, B_g.astype(_BF))
            b_t = B_g.T.astype(_BF)
            ys = []
            for pr in range(HG // 2):
                h0 = g * HG + 2 * pr
                h1 = h0 + 1
                cols = slice(h0 * P, (h0 + 2) * P)
                xs2 = xs[:, cols]
                pick = lambda v: jnp.where(lo, v[:, h0:h0 + 1], v[:, h1:h1 + 1])
                prev2 = state_ref[:, cols]
                y_off = _dot(C_g, prev2.astype(_BF)) * pick(e_col)
                ms = []
                for hh in (h0, h1):
                    seg = a_cs[:, hh:hh + 1] - a_cs_t[hh:hh + 1, :]
                    lmat = jnp.exp(jnp.where(tril, seg, -jnp.inf))
                    ms.append((cb * lmat * dt_t[hh:hh + 1, :]).astype(_BF))
                xs_lo = jnp.where(lo, xs2, 0.0).astype(_BF)
                xs_hi = jnp.where(lo, 0.0, xs2).astype(_BF)
                y_diag = _dot(jnp.concatenate(ms, axis=1), jnp.concatenate([xs_lo, xs_hi], axis=0))
                new_state = _dot(b_t, (xs2 * pick(w_col)).astype(_BF))
                state_ref[:, cols] = prev2 * pick(c_dec) + new_state
                y2 = y_diag + y_off + xs2 * dsk_ref[:, cols]
                ys.append(y2 * _silu(z_ref[rows, cols]))
            yg = jnp.concatenate(ys, axis=1)
            y_ref[rows, g * GW:(g + 1) * GW] = _rms(yg, nw_ref[:, g * GW:(g + 1) * GW]).astype(_BF)


def _ssd(z, xbc, dt_raw, cw, cb, dtb, alog, dsk, nw, *, B, S):
    T, DI = z.shape
    C = xbc.shape[1]
    L = SSD_CHUNK
    H, G, N = SSD_HEADS, SSD_GROUPS, SSD_STATE
    P = DI // H
    assert 2 * P == LANES and H % (2 * G) == 0 and S % L == 0
    nsub = math.gcd(SSD_CHUNKS_PER_STEP, S // L)
    nc = S // (L * nsub)
    row = lambda n: pl.BlockSpec((nsub * L, n), lambda b, c: (b * nc + c, 0))
    consts = (cw, cb, dtb, alog, dsk, nw)
    return pl.pallas_call(
        functools.partial(_ssd_kernel, L=L, H=H, P=P, G=G, N=N, nsub=nsub),
        grid=(B, nc),
        in_specs=[row(DI), row(C), row(LANES)] + [_const_spec(a.shape) for a in consts],
        out_specs=row(DI),
        out_shape=jax.ShapeDtypeStruct((T, DI), _BF),
        scratch_shapes=[pltpu.VMEM((SUBLANES, C), _F32), pltpu.VMEM((N, DI), _F32)],
        compiler_params=_params("arbitrary", "arbitrary"),
        name="ssd",
    )(z, xbc, dt_raw, *consts)


def _sb_kernel(q_ref, k_ref, v_ref, nw_ref, o_ref, acc_ref, r_ref, z_ref, p_ref, *, tq, hd, npp, nqb):
    lane = lax.broadcasted_iota(jnp.int32, (1, 2 * hd), 1)
    lo = lane < hd
    nh = 2 * npp
    ri = lax.broadcasted_iota(jnp.int32, (tq, tq), 0)
    ci = lax.broadcasted_iota(jnp.int32, (tq, tq), 1)
    tri = jnp.where(ri >= ci, 1.0, 0.0).astype(_BF)
    tri2 = jnp.concatenate([tri, tri], axis=0)
    rs = lax.broadcasted_iota(jnp.int32, (nh * tq, tq), 0) & (tq - 1)
    cs = lax.broadcasted_iota(jnp.int32, (nh * tq, tq), 1)
    causal = cs < rs
    first_tile = pl.program_id(2) * nqb

    def key_start(j):
        return pl.multiple_of(jnp.maximum(j, 0) * tq, tq)

    def weights(mode, z=None):
        nr = SB_TOP_ROWS if mode == "top" else tq

        def head_rows(ref):
            if mode != "top":
                return ref[...]
            return jnp.concatenate([ref[h * tq:h * tq + nr, :] for h in range(nh)], axis=0)

        if z is None:
            z = head_rows(z_ref)
        if mode == "diag":
            z = jnp.where(causal, z, -jnp.inf)
        lnb = -_softplus(z)
        r = head_rows(r_ref)
        one = jnp.int32(1)
        if mode == "diag":
            live_top, live_rest = one, one
        else:
            tot = r[:, 0:1] + jnp.sum(lnb, axis=-1, keepdims=True)
            if mode == "top":
                live_top, live_rest = (jnp.max(tot) > SB_SKIP_LOG).astype(jnp.int32), jnp.int32(0)
            else:
                pick = lambda lo_row, hi_row: jnp.concatenate(
                    [tot[h * tq + lo_row:h * tq + hi_row] for h in range(nh)], axis=0)
                live_top = (jnp.max(pick(0, SB_TOP_ROWS)) > SB_SKIP_LOG).astype(jnp.int32)
                live_rest = (jnp.max(pick(SB_TOP_ROWS, tq)) > SB_SKIP_LOG).astype(jnp.int32)
        hi = lnb.astype(_BF)
        lw = (lnb - hi.astype(_F32)).astype(_BF)
        cum = _dot(jnp.concatenate([hi, lw], axis=1), tri2)
        a = jnp.exp(z + cum + r)
        r_new = r + cum[:, 0:1]
        if mode == "top":
            for h in range(nh):
                r_ref[h * tq:h * tq + nr, :] = r_new[h * nr:(h + 1) * nr]
        else:
            r_ref[...] = r_new
        a = a.astype(_BF)
        for p in range(npp):
            p_ref[p, 0:nr, :] = jnp.concatenate(
                [a[2 * p * nr:(2 * p + 1) * nr], a[(2 * p + 1) * nr:(2 * p + 2) * nr]], axis=1)
            if mode == "top":
                p_ref[p, nr:tq, :] = jnp.zeros((tq - nr, 2 * tq), _BF)
        return live_top, live_rest

    def apply_weights(j):
        start = key_start(j)
        for p in range(npp):
            cols = slice(p * LANES, (p + 1) * LANES)
            v2 = v_ref[pl.ds(start, tq), cols]
            zv = jnp.zeros_like(v2)
            vv = jnp.concatenate([jnp.where(lo, v2, zv), jnp.where(lo, zv, v2)], axis=0)
            acc_ref[:, cols] += _dot(p_ref[p], vv)

    def finish(qb):
        rows = pl.ds(pl.multiple_of(qb * tq, tq), tq)
        for p in range(npp):
            cols = slice(p * LANES, (p + 1) * LANES)
            acc = acc_ref[:, cols]
            sq = acc * acc
            s_lo = jnp.sum(jnp.where(lo, sq, 0.0), axis=-1, keepdims=True)
            s_hi = jnp.sum(jnp.where(lo, 0.0, sq), axis=-1, keepdims=True)
            inv = lax.rsqrt(jnp.where(lo, s_lo, s_hi) * (1.0 / hd) + EPS)
            o_ref[rows, cols] = (acc * inv * nw_ref[:, cols]).astype(_BF)

    acc_ref[...] = jnp.zeros_like(acc_ref)
    p_ref[...] = jnp.zeros_like(p_ref)

    def query_tile(qb, pending):
        i = first_tile + qb
        rows = pl.ds(pl.multiple_of(qb * tq, tq), tq)
        qcat = []
        for p in range(npp):
            q2 = q_ref[rows, p * LANES:(p + 1) * LANES]
            zq = jnp.zeros_like(q2)
            qcat.append(jnp.concatenate([jnp.where(lo, q2, zq), jnp.where(lo, zq, q2)], axis=0))

        def scores(j):
            start = key_start(j)
            return jnp.concatenate(
                [_dot_nt(qcat[p], k_ref[pl.ds(start, tq), p * LANES:(p + 1) * LANES]) for p in range(npp)],
                axis=0)

        def start_walk(with_left_tile):
            def run():
                z_first = scores(i)
                z_left = scores(i - 1)
                apply_weights(pending)
                finish(jnp.maximum(qb - 1, 0))
                acc_ref[...] = jnp.zeros_like(acc_ref)
                r_ref[...] = jnp.zeros_like(r_ref)
                live = weights("diag", z_first)
                if not with_left_tile:
                    z_ref[...] = z_left
                    return (i - 1,) + live
                z_next = scores(i - 2)
                apply_weights(i)
                live = weights("full", z_left)
                z_ref[...] = z_next
                return (i - 2,) + live
            return run

        first = lax.cond(i > 0, start_walk(True), start_walk(False))

        def cond(carry):
            j, live_top, live_rest = carry
            return jnp.logical_and(j >= 0, (live_top + live_rest) > 0)

        def body(carry):
            j, _, live_rest = carry

            def step(mode):
                def run():
                    apply_weights(j + 1)
                    z_next = scores(j - 1)
                    live = weights(mode)
                    z_ref[...] = z_next
                    return live
                return run

            live_top, live_rest = lax.cond(live_rest > 0, step("full"), step("top"))
            return j - 1, live_top, live_rest

        j_end, _, _ = lax.while_loop(cond, body, first)
        return j_end + 1

    pending = lax.fori_loop(0, nqb, query_tile, jnp.int32(0))
    apply_weights(pending)
    finish(nqb - 1)


def _sb_attention(q, k, v, nw, *, B, S):
    T, W = q.shape
    hd = SB_HEAD_DIM
    tq = SB_TILE
    npp = SB_PAIRS_PER_STEP
    cw = npp * LANES
    assert 2 * hd == LANES and tq == LANES and S % tq == 0 and W % cw == 0
    nqb = math.gcd(SB_QTILES_PER_STEP, S // tq)
    nq = S // (tq * nqb)
    k3 = k.reshape(B, S, W)
    v3 = v.reshape(B, S, W)
    return pl.pallas_call(
        functools.partial(_sb_kernel, tq=tq, hd=hd, npp=npp, nqb=nqb),
        grid=(B, W // cw, nq),
        in_specs=[
            pl.BlockSpec((nqb * tq, cw), lambda b, p, i: (b * nq + i, p)),
            pl.BlockSpec((None, S, cw), lambda b, p, i: (b, 0, p), pipeline_mode=pl.Buffered(1)),
            pl.BlockSpec((None, S, cw), lambda b, p, i: (b, 0, p), pipeline_mode=pl.Buffered(1)),
            pl.BlockSpec((1, cw), lambda b, p, i: (0, p)),
        ],
        out_specs=pl.BlockSpec((nqb * tq, cw), lambda b, p, i: (b * nq + i, p)),
        out_shape=jax.ShapeDtypeStruct((T, W), _BF),
        scratch_shapes=[pltpu.VMEM((tq, cw), _F32), pltpu.VMEM((2 * npp * tq, tq), _F32),
                        pltpu.VMEM((2 * npp * tq, tq), _F32), pltpu.VMEM((npp, tq, 2 * tq), _BF)],
        compiler_params=_params("arbitrary", "arbitrary", "arbitrary"),
        name="sb_attn",
    )(q, k3, v3, nw)


def _memkv_kernel(m_ref, nw_ref, wk_ref, wv_ref, k_ref, v_ref):
    mb = _rms(m_ref[...], nw_ref[...]).astype(_BF)
    k_ref[...] = _dot(mb, wk_ref[...]).astype(_BF)
    v_ref[...] = _dot(mb, wv_ref[...]).astype(_BF)


def _mem_kv(mem2d, nw, wk, wv):
    R, D = mem2d.shape
    return pl.pallas_call(
        _memkv_kernel,
        grid=(1,),
        in_specs=[_const_spec(a.shape) for a in (mem2d, nw, wk, wv)],
        out_specs=[_const_spec((R, D))] * 2,
        out_shape=[jax.ShapeDtypeStruct((R, D), _BF)] * 2,
        compiler_params=_params("arbitrary"),
        name="mem_kv",
    )(mem2d, nw, wk, wv)


def _mixmem_kernel(x_ref, ya_ref, yb_ref, wo1_ref, wo2_ref, nw_ref, wq_ref, mk_ref, mv_ref, wo_ref,
                   o_ref, *, heads, q_scale):
    x1 = x_ref[...] + _dot(ya_ref[...], wo1_ref[...]) + _dot(yb_ref[...], wo2_ref[...])
    hb = _rms(x1, nw_ref[...]).astype(_BF)
    q = (_dot(hb, wq_ref[...]) * q_scale).astype(_BF)
    D = q.shape[1]
    hd = D // heads
    outs = []
    for h in range(heads):
        cols = slice(h * hd, (h + 1) * hd)
        sc = _dot_nt(q[:, cols], mk_ref[:, cols])
        sc = sc - jnp.max(sc, axis=-1, keepdims=True)
        e = jnp.exp(sc)
        p = e / jnp.sum(e, axis=-1, keepdims=True)
        outs.append(_dot(p.astype(_BF), mv_ref[:, cols]).astype(_BF))
    o = jnp.concatenate(outs, axis=1)
    o_ref[...] = x1 + _dot(o, wo_ref[...])


def _mix_mem(x2d, ya, yb, wo1, wo2, nw, wq, mk, mv, wo, *, B, S, tm):
    T, D = x2d.shape
    M = mk.shape[0] // B
    nt = S // tm
    row = pl.BlockSpec((tm, D), lambda b, i: (b * nt + i, 0))
    memspec = pl.BlockSpec((M, D), lambda b, i: (b, 0))
    hd = D // MEM_HEADS
    return pl.pallas_call(
        functools.partial(_mixmem_kernel, heads=MEM_HEADS, q_scale=1.0 / math.sqrt(hd)),
        grid=(B, nt),
        in_specs=[row, row, row, _const_spec(wo1.shape), _const_spec(wo2.shape), _const_spec(nw.shape),
                  _const_spec(wq.shape), memspec, memspec, _const_spec(wo.shape)],
        out_specs=row,
        out_shape=jax.ShapeDtypeStruct((T, D), _F32),
        compiler_params=_params("arbitrary", "arbitrary"),
        name="mix_mem",
    )(x2d, ya, yb, wo1, wo2, nw, wq, mk, mv, wo)


def _ffn_kernel(x_ref, halo_ref, nw_ref, wup_ref, cw_ref, cb_ref, wdn_ref, fw_ref,
                o_ref, acc_ref, h_ref, ua_ref, ub_ref, aa_ref, ab_ref, *, tm, fc, dff, final_norm):
    i = pl.program_id(1)
    x = x_ref[...]
    keep = (i > 0).astype(_F32)
    h_ref[...] = jnp.concatenate(
        [_rms(halo_ref[...], nw_ref[...]) * keep, _rms(x, nw_ref[...])], axis=0).astype(_BF)
    acc_ref[...] = jnp.zeros_like(acc_ref)

    ubuf = (ua_ref, ub_ref)
    abuf = (aa_ref, ab_ref)
    starts = list(range(0, dff, fc))
    widths = [min(fc, dff - s0) for s0 in starts]
    nchunk = len(starts)

    def cols(c, base):
        return slice(base + starts[c], base + starts[c] + widths[c])

    def up(c):
        w = widths[c]
        ubuf[c % 2][0, :, :w] = _dot(h_ref[...], wup_ref[:, cols(c, 0)])
        ubuf[c % 2][1, :, :w] = _dot(h_ref[...], wup_ref[:, cols(c, dff)])

    def conv(u, cw, cb):
        out = cb + cw[FFN_CONV - 1:FFN_CONV, :] * u
        for kk in range(1, FFN_CONV):
            out = out + cw[FFN_CONV - 1 - kk:FFN_CONV - kk, :] * pltpu.roll(u, kk, 0)
        return out[FFN_HALO:]

    def gate(c):
        w = widths[c]
        g = conv(ubuf[c % 2][0, :, :w], cw_ref[:, cols(c, 0)], cb_ref[:, cols(c, 0)])
        val = conv(ubuf[c % 2][1, :, :w], cw_ref[:, cols(c, dff)], cb_ref[:, cols(c, dff)])
        abuf[c % 2][:, :w] = (_silu(g) * val).astype(_BF)

    def down(c):
        acc_ref[...] += _dot(abuf[c % 2][:, :widths[c]], wdn_ref[cols(c, 0), :])

    up(0)
    if nchunk > 1:
        up(1)
    gate(0)
    for c in range(nchunk):
        down(c)
        if c + 1 < nchunk:
            gate(c + 1)
        if c + 2 < nchunk:
            up(c + 2)
    y = x + acc_ref[...]
    o_ref[...] = _rms(y, fw_ref[...]) if final_norm else y


def _ffn(x2d, nw, wup, cw, cb, wdn, fw, *, B, S, tm, final_norm):
    T, D = x2d.shape
    dff = wdn.shape[0]
    fc = FFN_CHUNK
    assert dff % LANES == 0 and tm % FFN_HALO == 0
    nt = S // tm
    hb = tm // FFN_HALO
    row = pl.BlockSpec((tm, D), lambda b, i: (b * nt + i, 0))
    halo = pl.BlockSpec((FFN_HALO, D), lambda b, i: (jnp.maximum((b * nt + i) * hb - 1, 0), 0))
    return pl.pallas_call(
        functools.partial(_ffn_kernel, tm=tm, fc=fc, dff=dff, final_norm=final_norm),
        grid=(B, nt),
        in_specs=[row, halo] + [_const_spec(a.shape) for a in (nw, wup, cw, cb, wdn, fw)],
        out_specs=row,
        out_shape=jax.ShapeDtypeStruct((T, D), _F32),
        scratch_shapes=[pltpu.VMEM((tm, D), _F32),
                        pltpu.VMEM((tm + FFN_HALO, D), _BF),
                        pltpu.VMEM((2, tm + FFN_HALO, fc), _F32),
                        pltpu.VMEM((2, tm + FFN_HALO, fc), _F32),
                        pltpu.VMEM((tm, fc), _BF),
                        pltpu.VMEM((tm, fc), _BF)],
        compiler_params=_params("arbitrary", "arbitrary"),
        name="ffn",
    )(x2d, x2d, nw, wup, cw, cb, wdn, fw)


def _pad_lanes(a):
    return jnp.pad(a, ((0, 0), (0, LANES - a.shape[1])))


def kernel(x, mem, norm_mix_w, w_in, conv_ssd_w, conv_ssd_b, dt_bias, a_log, d_skip, ssd_norm_w, sb_norm_w, w_out, norm_mem_w, norm_memkv_w, w_mq, w_mk, w_mv, w_mo, norm_ffn_w, w_up, conv_ffn_w, conv_ffn_b, w_down, norm_final_w):
    B, S, D = x.shape
    T = B * S
    depth = w_in.shape[0]
    H = SSD_HEADS
    DI = ssd_norm_w.shape[1]
    C = conv_ssd_w.shape[2]
    W = sb_norm_w.shape[1]
    o1, o2 = DI, DI + C
    o3 = o2 + H
    o4, o5 = o3 + W, o3 + 2 * W

    x2d = x.reshape(T, D)
    mem2d = mem.reshape(B * mem.shape[1], D)
    for l in range(depth):
        bf = lambda a: a.astype(_BF)
        z, xbc, dt_raw, q, k, v = _in_proj(x2d, norm_mix_w[l][None], bf(w_in[l]), (o1, o2, o3, o4, o5, o5 + W),
                                           tm=min(IN_PROJ_ROWS, S))
        y_ssd = _ssd(z, xbc, dt_raw, conv_ssd_w[l], conv_ssd_b[l][None], _pad_lanes(dt_bias[l][None]),
                     _pad_lanes(a_log[l][None]), jnp.repeat(d_skip[l], DI // H)[None], ssd_norm_w[l][None],
                     B=B, S=S)
        y_sb = _sb_attention(q, k, v, sb_norm_w[l][None], B=B, S=S)
        mk, mv = _mem_kv(mem2d, norm_memkv_w[l][None], bf(w_mk[l]), bf(w_mv[l]))
        x2d = _mix_mem(x2d, y_ssd, y_sb, bf(w_out[l][:DI]), bf(w_out[l][DI:]), norm_mem_w[l][None],
                       bf(w_mq[l]), mk, mv, bf(w_mo[l]), B=B, S=S, tm=min(MIX_ROWS, S))
        x2d = _ffn(x2d, norm_ffn_w[l][None], bf(w_up[l]), conv_ffn_w[l], conv_ffn_b[l][None], bf(w_down[l]),
                   norm_final_w[None], B=B, S=S, tm=min(FFN_ROWS, S), final_norm=(l == depth - 1))
    return x2d.reshape(B, S, D)
```

```python
import functools
import math

import jax
import jax.numpy as jnp
from jax import lax
from jax.experimental import pallas as pl
from jax.experimental.pallas import tpu as pltpu

EPS = 1e-6
LANES = 128
SUBLANES = 8
VMEM_LIMIT = 56 * 1024 * 1024

SSD_HEADS = 16
SSD_GROUPS = 2
SSD_STATE = 128
SSD_CONV = 4
SSD_CHUNK = 128
SSD_CHUNKS_PER_STEP = 4
SB_HEAD_DIM = 64
SB_TILE = 128
SB_PAIRS_PER_STEP = 8
SB_TOP_ROWS = 48
SB_QTILES_PER_STEP = 8
MEM_HEADS = 4
IN_PROJ_ROWS = 512
MIX_ROWS = 1024
FFN_ROWS = 512
FFN_CONV = 3
FFN_CHUNK = 768
FFN_HALO = 16

SB_SKIP_LOG = -110.0

_BF = jnp.bfloat16
_F32 = jnp.float32


def _dot(a, b):
    return jnp.dot(a, b, preferred_element_type=_F32)


def _dot_nt(a, b):
    return lax.dot_general(a, b, (((1,), (1,)), ((), ())), preferred_element_type=_F32)


def _rms(x, w):
    return x * lax.rsqrt(jnp.mean(x * x, axis=-1, keepdims=True) + EPS) * w


def _softplus(x):
    return jnp.maximum(x, 0.0) + jnp.log(1.0 + jnp.exp(-jnp.abs(x)))


def _silu(x):
    return x * jax.nn.sigmoid(x)


def _params(*sem):
    return pltpu.CompilerParams(dimension_semantics=sem, vmem_limit_bytes=VMEM_LIMIT)


def _const_spec(shape):
    nd = len(shape)
    return pl.BlockSpec(shape, lambda *_: (0,) * nd, pipeline_mode=pl.Buffered(1))


def _inproj_kernel(x_ref, nw_ref, w_ref, z_ref, xbc_ref, dt_ref, q_ref, k_ref, v_ref,
                   wq_ref, wk_ref, wv_ref, *, q_scale, offs):
    o1, o2, o3, o4, o5, o6 = offs

    @pl.when(pl.program_id(0) == 0)
    def _():
        wq_ref[...] = w_ref[:, o3:o4]
        wk_ref[...] = w_ref[:, o4:o5]
        wv_ref[...] = w_ref[:, o5:o6]

    hb = _rms(x_ref[...], nw_ref[...]).astype(_BF)
    z_ref[...] = _dot(hb, w_ref[:, 0:o1])
    xbc_ref[...] = _dot(hb, w_ref[:, o1:o2])
    dt_ref[...] = _dot(hb, w_ref[:, o2:o2 + LANES])
    q_ref[...] = (_dot(hb, wq_ref[...]) * q_scale).astype(_BF)
    k_ref[...] = _dot(hb, wk_ref[...]).astype(_BF)
    v_ref[...] = _dot(hb, wv_ref[...]).astype(_BF)


def _in_proj(x2d, nw, w, offs, *, tm):
    T, D = x2d.shape
    o1, o2, o3, o4, o5, o6 = offs
    assert o1 % LANES == 0 and o2 % LANES == 0 and o2 + LANES <= o6
    row = lambda n: pl.BlockSpec((tm, n), lambda i: (i, 0))
    widths = (o1, o2 - o1, LANES, o4 - o3, o5 - o4, o6 - o5)
    out_dtypes = (_F32, _F32, _F32, _BF, _BF, _BF)
    return pl.pallas_call(
        functools.partial(_inproj_kernel, q_scale=1.0 / math.sqrt(SB_HEAD_DIM), offs=offs),
        grid=(T // tm,),
        in_specs=[row(D), _const_spec(nw.shape), _const_spec(w.shape)],
        out_specs=[row(n) for n in widths],
        out_shape=[jax.ShapeDtypeStruct((T, n), dt) for n, dt in zip(widths, out_dtypes)],
        scratch_shapes=[pltpu.VMEM((D, n), _BF) for n in widths[3:]],
        compiler_params=_params("arbitrary"),
        name="in_proj",
    )(x2d, nw, w)


def _split3(v):
    v1 = v.astype(_BF)
    r1 = v - v1.astype(_F32)
    v2 = r1.astype(_BF)
    v3 = (r1 - v2.astype(_F32)).astype(_BF)
    return v1, v2, v3


def _ssd_kernel(z_ref, xbc_ref, dt_ref, cw_ref, cb_ref, dtb_ref, alog_ref, dsk_ref, nw_ref,
                y_ref, tail_ref, state_ref, *, L, H, P, G, N, nsub):
    DI = H * P
    HG = H // G
    GW = DI // G
    R = nsub * L
    c = pl.program_id(1)

    @pl.when(c == 0)
    def _():
        tail_ref[...] = jnp.zeros_like(tail_ref)
        state_ref[...] = jnp.zeros_like(state_ref)

    xbc = xbc_ref[...]
    ext = jnp.concatenate([tail_ref[...], xbc], axis=0)
    tail_ref[...] = xbc[R - SUBLANES:, :]
    conv = cb_ref[...] + cw_ref[SSD_CONV - 1:SSD_CONV, :] * xbc
    for kk in range(1, SSD_CONV):
        conv = conv + cw_ref[SSD_CONV - 1 - kk:SSD_CONV - kk, :] * pltpu.roll(ext, kk, 0)[SUBLANES:]
    act_all = _silu(conv)
    dt_all = _softplus(dt_ref[...] + dtb_ref[...])
    a_all = dt_all * (-jnp.exp(alog_ref[...]))

    ri = lax.broadcasted_iota(jnp.int32, (L, L), 0)
    ci = lax.broadcasted_iota(jnp.int32, (L, L), 1)
    tril = ri >= ci
    tril_bf = jnp.where(tril, 1.0, 0.0).astype(_BF)
    lane = lax.broadcasted_iota(jnp.int32, (1, 2 * P), 1)
    lo = lane < P

    for sc in range(nsub):
        rows = slice(sc * L, (sc + 1) * L)
        act = act_all[rows]
        xs = act[:, :DI]
        Bm = act[:, DI:DI + G * N]
        Cm = act[:, DI + G * N:]
        dt = dt_all[rows]
        a1, a2, a3 = _split3(a_all[rows])
        a_cs = _dot(tril_bf, a1) + _dot(tril_bf, a2) + _dot(tril_bf, a3)
        a_last = a_cs[L - 1:L, :]
        a_cs_t = a_cs.T
        dt_t = dt.T
        w_col = jnp.exp(a_last - a_cs) * dt
        e_col = jnp.exp(a_cs)
        c_dec = jnp.exp(a_last)

        for g in range(G):
            B_g = Bm[:, g * N:(g + 1) * N]
            C_g = Cm[:, g * N:(g + 1) * N].astype(_BF)
            cb = _dot_nt(C_g, B_g.astype(_BF))
            b_t = B_g.T.astype(_BF)
            ys = []
            for pr in range(HG // 2):
                h0 = g * HG + 2 * pr
                h1 = h0 + 1
                cols = slice(h0 * P, (h0 + 2) * P)
                xs2 = xs[:, cols]
                pick = lambda v: jnp.where(lo, v[:, h0:h0 + 1], v[:, h1:h1 + 1])
                prev2 = state_ref[:, cols]
                y_off = _dot(C_g, prev2.astype(_BF)) * pick(e_col)
                ms = []
                for hh in (h0, h1):
                    seg = a_cs[:, hh:hh + 1] - a_cs_t[hh:hh + 1, :]
                    lmat = jnp.exp(jnp.where(tril, seg, -jnp.inf))
                    ms.append((cb * lmat * dt_t[hh:hh + 1, :]).astype(_BF))
                xs_lo = jnp.where(lo, xs2, 0.0).astype(_BF)
                xs_hi = jnp.where(lo, 0.0, xs2).astype(_BF)
                y_diag = _dot(jnp.concatenate(ms, axis=1), jnp.concatenate([xs_lo, xs_hi], axis=0))
                new_state = _dot(b_t, (xs2 * pick(w_col)).astype(_BF))
                state_ref[:, cols] = prev2 * pick(c_dec) + new_state
                y2 = y_diag + y_off + xs2 * dsk_ref[:, cols]
                ys.append(y2 * _silu(z_ref[rows, cols]))
            yg = jnp.concatenate(ys, axis=1)
            y_ref[rows, g * GW:(g + 1) * GW] = _rms(yg, nw_ref[:, g * GW:(g + 1) * GW]).astype(_BF)


def _ssd(z, xbc, dt_raw, cw, cb, dtb, alog, dsk, nw, *, B, S):
    T, DI = z.shape
    C = xbc.shape[1]
    L = SSD_CHUNK
    H, G, N = SSD_HEADS, SSD_GROUPS, SSD_STATE
    P = DI // H
    assert 2 * P == LANES and H % (2 * G) == 0 and S % L == 0
    nsub = math.gcd(SSD_CHUNKS_PER_STEP, S // L)
    nc = S // (L * nsub)
    row = lambda n: pl.BlockSpec((nsub * L, n), lambda b, c: (b * nc + c, 0))
    consts = (cw, cb, dtb, alog, dsk, nw)
    return pl.pallas_call(
        functools.partial(_ssd_kernel, L=L, H=H, P=P, G=G, N=N, nsub=nsub),
        grid=(B, nc),
        in_specs=[row(DI), row(C), row(LANES)] + [_const_spec(a.shape) for a in consts],
        out_specs=row(DI),
        out_shape=jax.ShapeDtypeStruct((T, DI), _BF),
        scratch_shapes=[pltpu.VMEM((SUBLANES, C), _F32), pltpu.VMEM((N, DI), _F32)],
        compiler_params=_params("arbitrary", "arbitrary"),
        name="ssd",
    )(z, xbc, dt_raw, *consts)


def _sb_kernel(q_ref, k_ref, v_ref, nw_ref, o_ref, acc_ref, r_ref, z_ref, p_ref, *, tq, hd, npp, nqb):
    lane = lax.broadcasted_iota(jnp.int32, (1, 2 * hd), 1)
    lo = lane < hd
    nh = 2 * npp
    ri = lax.broadcasted_iota(jnp.int32, (tq, tq), 0)
    ci = lax.broadcasted_iota(jnp.int32, (tq, tq), 1)
    tri = jnp.where(ri >= ci, 1.0, 0.0).astype(_BF)
    tri2 = jnp.concatenate([tri, tri], axis=0)
    rs = lax.broadcasted_iota(jnp.int32, (nh * tq, tq), 0) & (tq - 1)
    cs = lax.broadcasted_iota(jnp.int32, (nh * tq, tq), 1)
    causal = cs < rs
    first_tile = pl.program_id(2) * nqb

    def key_start(j):
        return pl.multiple_of(jnp.maximum(j, 0) * tq, tq)

    def weights(mode, z=None):
        nr = SB_TOP_ROWS if mode == "top" else tq

        def head_rows(ref):
            if mode != "top":
                return ref[...]
            return jnp.concatenate([ref[h * tq:h * tq + nr, :] for h in range(nh)], axis=0)

        if z is None:
            z = head_rows(z_ref)
        if mode == "diag":
            z = jnp.where(causal, z, -jnp.inf)
        lnb = -_softplus(z)
        r = head_rows(r_ref)
        one = jnp.int32(1)
        if mode == "diag":
            live_top, live_rest = one, one
        else:
            tot = r[:, 0:1] + jnp.sum(lnb, axis=-1, keepdims=True)
            if mode == "top":
                live_top, live_rest = (jnp.max(tot) > SB_SKIP_LOG).astype(jnp.int32), jnp.int32(0)
            else:
                pick = lambda lo_row, hi_row: jnp.concatenate(
                    [tot[h * tq + lo_row:h * tq + hi_row] for h in range(nh)], axis=0)
                live_top = (jnp.max(pick(0, SB_TOP_ROWS)) > SB_SKIP_LOG).astype(jnp.int32)
                live_rest = (jnp.max(pick(SB_TOP_ROWS, tq)) > SB_SKIP_LOG).astype(jnp.int32)
        hi = lnb.astype(_BF)
        lw = (lnb - hi.astype(_F32)).astype(_BF)
        cum = _dot(jnp.concatenate([hi, lw], axis=1), tri2)
        a = jnp.exp(z + cum + r)
        r_new = r + cum[:, 0:1]
        if mode == "top":
            for h in range(nh):
                r_ref[h * tq:h * tq + nr, :] = r_new[h * nr:(h + 1) * nr]
        else:
            r_ref[...] = r_new
        a = a.astype(_BF)
        for p in range(npp):
            p_ref[p, 0:nr, :] = jnp.concatenate(
                [a[2 * p * nr:(2 * p + 1) * nr], a[(2 * p + 1) * nr:(2 * p + 2) * nr]], axis=1)
            if mode == "top":
                p_ref[p, nr:tq, :] = jnp.zeros((tq - nr, 2 * tq), _BF)
        return live_top, live_rest

    def apply_weights(j):
        start = key_start(j)
        for p in range(npp):
            cols = slice(p * LANES, (p + 1) * LANES)
            v2 = v_ref[pl.ds(start, tq), cols]
            zv = jnp.zeros_like(v2)
            vv = jnp.concatenate([jnp.where(lo, v2, zv), jnp.where(lo, zv, v2)], axis=0)
            acc_ref[:, cols] += _dot(p_ref[p], vv)

    def finish(qb):
        rows = pl.ds(pl.multiple_of(qb * tq, tq), tq)
        for p in range(npp):
            cols = slice(p * LANES, (p + 1) * LANES)
            acc = acc_ref[:, cols]
            sq = acc * acc
            s_lo = jnp.sum(jnp.where(lo, sq, 0.0), axis=-1, keepdims=True)
            s_hi = jnp.sum(jnp.where(lo, 0.0, sq), axis=-1, keepdims=True)
            inv = lax.rsqrt(jnp.where(lo, s_lo, s_hi) * (1.0 / hd) + EPS)
            o_ref[rows, cols] = (acc * inv * nw_ref[:, cols]).astype(_BF)

    acc_ref[...] = jnp.zeros_like(acc_ref)
    p_ref[...] = jnp.zeros_like(p_ref)

    def query_tile(qb, pending):
        i = first_tile + qb
        rows = pl.ds(pl.multiple_of(qb * tq, tq), tq)
        qcat = []
        for p in range(npp):
            q2 = q_ref[rows, p * LANES:(p + 1) * LANES]
            zq = jnp.zeros_like(q2)
            qcat.append(jnp.concatenate([jnp.where(lo, q2, zq), jnp.where(lo, zq, q2)], axis=0))

        def scores(j):
            start = key_start(j)
            return jnp.concatenate(
                [_dot_nt(qcat[p], k_ref[pl.ds(start, tq), p * LANES:(p + 1) * LANES]) for p in range(npp)],
                axis=0)

        z_first = scores(i)
        z_left = jnp.where(i > 0, scores(i - 1), -jnp.inf)
        z_next = scores(i - 2)
        apply_weights(pending)
        finish(jnp.maximum(qb - 1, 0))
        acc_ref[...] = jnp.zeros_like(acc_ref)
        r_ref[...] = jnp.zeros_like(r_ref)
        weights("diag", z_first)
        apply_weights(i)
        live_top, live_rest = weights("full", z_left)
        z_ref[...] = z_next

        def step(mode):
            def body(carry):
                j = carry[0]
                apply_weights(j + 1)
                z_next = scores(j - 1)
                live = weights(mode)
                z_ref[...] = z_next
                return (j - 1,) + (live if mode == "full" else live[:1])
            return body

        j_mid, live_top, _ = lax.while_loop(
            lambda c: jnp.logical_and(c[0] >= 0, c[2] > 0), step("full"), (i - 2, live_top, live_rest))
        j_end, _ = lax.while_loop(
            lambda c: jnp.logical_and(c[0] >= 0, c[1] > 0), step("top"), (j_mid, live_top))
        return j_end + 1

    pending = lax.fori_loop(0, nqb, query_tile, jnp.int32(0))
    apply_weights(pending)
    finish(nqb - 1)


def _sb_attention(q, k, v, nw, *, B, S):
    T, W = q.shape
    hd = SB_HEAD_DIM
    tq = SB_TILE
    npp = SB_PAIRS_PER_STEP
    cw = npp * LANES
    assert 2 * hd == LANES and tq == LANES and S % tq == 0 and W % cw == 0
    nqb = math.gcd(SB_QTILES_PER_STEP, S // tq)
    nq = S // (tq * nqb)
    k3 = k.reshape(B, S, W)
    v3 = v.reshape(B, S, W)
    return pl.pallas_call(
        functools.partial(_sb_kernel, tq=tq, hd=hd, npp=npp, nqb=nqb),
        grid=(B, W // cw, nq),
        in_specs=[
            pl.BlockSpec((nqb * tq, cw), lambda b, p, i: (b * nq + i, p)),
            pl.BlockSpec((None, S, cw), lambda b, p, i: (b, 0, p), pipeline_mode=pl.Buffered(1)),
            pl.BlockSpec((None, S, cw), lambda b, p, i: (b, 0, p), pipeline_mode=pl.Buffered(1)),
            pl.BlockSpec((1, cw), lambda b, p, i: (0, p)),
        ],
        out_specs=pl.BlockSpec((nqb * tq, cw), lambda b, p, i: (b * nq + i, p)),
        out_shape=jax.ShapeDtypeStruct((T, W), _BF),
        scratch_shapes=[pltpu.VMEM((tq, cw), _F32), pltpu.VMEM((2 * npp * tq, tq), _F32),
                        pltpu.VMEM((2 * npp * tq, tq), _F32), pltpu.VMEM((npp, tq, 2 * tq), _BF)],
        compiler_params=_params("arbitrary", "arbitrary", "arbitrary"),
        name="sb_attn",
    )(q, k3, v3, nw)


def _memkv_kernel(m_ref, nw_ref, wk_ref, wv_ref, k_ref, v_ref):
    mb = _rms(m_ref[...], nw_ref[...]).astype(_BF)
    k_ref[...] = _dot(mb, wk_ref[...]).astype(_BF)
    v_ref[...] = _dot(mb, wv_ref[...]).astype(_BF)


def _mem_kv(mem2d, nw, wk, wv):
    R, D = mem2d.shape
    return pl.pallas_call(
        _memkv_kernel,
        grid=(1,),
        in_specs=[_const_spec(a.shape) for a in (mem2d, nw, wk, wv)],
        out_specs=[_const_spec((R, D))] * 2,
        out_shape=[jax.ShapeDtypeStruct((R, D), _BF)] * 2,
        compiler_params=_params("arbitrary"),
        name="mem_kv",
    )(mem2d, nw, wk, wv)


def _mixmem_kernel(x_ref, ya_ref, yb_ref, wo1_ref, wo2_ref, nw_ref, wq_ref, mk_ref, mv_ref, wo_ref,
                   o_ref, *, heads, q_scale):
    x1 = x_ref[...] + _dot(ya_ref[...], wo1_ref[...]) + _dot(yb_ref[...], wo2_ref[...])
    hb = _rms(x1, nw_ref[...]).astype(_BF)
    q = (_dot(hb, wq_ref[...]) * q_scale).astype(_BF)
    D = q.shape[1]
    hd = D // heads
    outs = []
    for h in range(heads):
        cols = slice(h * hd, (h + 1) * hd)
        sc = _dot_nt(q[:, cols], mk_ref[:, cols])
        sc = sc - jnp.max(sc, axis=-1, keepdims=True)
        e = jnp.exp(sc)
        p = e / jnp.sum(e, axis=-1, keepdims=True)
        outs.append(_dot(p.astype(_BF), mv_ref[:, cols]).astype(_BF))
    o = jnp.concatenate(outs, axis=1)
    o_ref[...] = x1 + _dot(o, wo_ref[...])


def _mix_mem(x2d, ya, yb, wo1, wo2, nw, wq, mk, mv, wo, *, B, S, tm):
    T, D = x2d.shape
    M = mk.shape[0] // B
    nt = S // tm
    row = pl.BlockSpec((tm, D), lambda b, i: (b * nt + i, 0))
    memspec = pl.BlockSpec((M, D), lambda b, i: (b, 0))
    hd = D // MEM_HEADS
    return pl.pallas_call(
        functools.partial(_mixmem_kernel, heads=MEM_HEADS, q_scale=1.0 / math.sqrt(hd)),
        grid=(B, nt),
        in_specs=[row, row, row, _const_spec(wo1.shape), _const_spec(wo2.shape), _const_spec(nw.shape),
                  _const_spec(wq.shape), memspec, memspec, _const_spec(wo.shape)],
        out_specs=row,
        out_shape=jax.ShapeDtypeStruct((T, D), _F32),
        compiler_params=_params("arbitrary", "arbitrary"),
        name="mix_mem",
    )(x2d, ya, yb, wo1, wo2, nw, wq, mk, mv, wo)


def _ffn_kernel(x_ref, halo_ref, nw_ref, wup_ref, cw_ref, cb_ref, wdn_ref, fw_ref,
                o_ref, acc_ref, h_ref, ua_ref, ub_ref, aa_ref, ab_ref, *, tm, fc, dff, final_norm):
    i = pl.program_id(1)
    x = x_ref[...]
    keep = (i > 0).astype(_F32)
    h_ref[...] = jnp.concatenate(
        [_rms(halo_ref[...], nw_ref[...]) * keep, _rms(x, nw_ref[...])], axis=0).astype(_BF)
    acc_ref[...] = jnp.zeros_like(acc_ref)

    ubuf = (ua_ref, ub_ref)
    abuf = (aa_ref, ab_ref)
    starts = list(range(0, dff, fc))
    widths = [min(fc, dff - s0) for s0 in starts]
    nchunk = len(starts)

    def cols(c, base):
        return slice(base + starts[c], base + starts[c] + widths[c])

    def up(c):
        w = widths[c]
        ubuf[c % 2][0, :, :w] = _dot(h_ref[...], wup_ref[:, cols(c, 0)])
        ubuf[c % 2][1, :, :w] = _dot(h_ref[...], wup_ref[:, cols(c, dff)])

    def conv(u, cw, cb):
        out = cb + cw[FFN_CONV - 1:FFN_CONV, :] * u
        for kk in range(1, FFN_CONV):
            out = out + cw[FFN_CONV - 1 - kk:FFN_CONV - kk, :] * pltpu.roll(u, kk, 0)
        return out[FFN_HALO:]

    def gate(c):
        w = widths[c]
        g = conv(ubuf[c % 2][0, :, :w], cw_ref[:, cols(c, 0)], cb_ref[:, cols(c, 0)])
        val = conv(ubuf[c % 2][1, :, :w], cw_ref[:, cols(c, dff)], cb_ref[:, cols(c, dff)])
        abuf[c % 2][:, :w] = (_silu(g) * val).astype(_BF)

    def down(c):
        acc_ref[...] += _dot(abuf[c % 2][:, :widths[c]], wdn_ref[cols(c, 0), :])

    up(0)
    if nchunk > 1:
        up(1)
    gate(0)
    for c in range(nchunk):
        down(c)
        if c + 1 < nchunk:
            gate(c + 1)
        if c + 2 < nchunk:
            up(c + 2)
    y = x + acc_ref[...]
    o_ref[...] = _rms(y, fw_ref[...]) if final_norm else y


def _ffn(x2d, nw, wup, cw, cb, wdn, fw, *, B, S, tm, final_norm):
    T, D = x2d.shape
    dff = wdn.shape[0]
    fc = FFN_CHUNK
    assert dff % LANES == 0 and tm % FFN_HALO == 0
    nt = S // tm
    hb = tm // FFN_HALO
    row = pl.BlockSpec((tm, D), lambda b, i: (b * nt + i, 0))
    halo = pl.BlockSpec((FFN_HALO, D), lambda b, i: (jnp.maximum((b * nt + i) * hb - 1, 0), 0))
    return pl.pallas_call(
        functools.partial(_ffn_kernel, tm=tm, fc=fc, dff=dff, final_norm=final_norm),
        grid=(B, nt),
        in_specs=[row, halo] + [_const_spec(a.shape) for a in (nw, wup, cw, cb, wdn, fw)],
        out_specs=row,
        out_shape=jax.ShapeDtypeStruct((T, D), _F32),
        scratch_shapes=[pltpu.VMEM((tm, D), _F32),
                        pltpu.VMEM((tm + FFN_HALO, D), _BF),
                        pltpu.VMEM((2, tm + FFN_HALO, fc), _F32),
                        pltpu.VMEM((2, tm + FFN_HALO, fc), _F32),
                        pltpu.VMEM((tm, fc), _BF),
                        pltpu.VMEM((tm, fc), _BF)],
        compiler_params=_params("arbitrary", "arbitrary"),
        name="ffn",
    )(x2d, x2d, nw, wup, cw, cb, wdn, fw)


def _pad_lanes(a):
    return jnp.pad(a, ((0, 0), (0, LANES - a.shape[1])))


def kernel(x, mem, norm_mix_w, w_in, conv_ssd_w, conv_ssd_b, dt_bias, a_log, d_skip, ssd_norm_w, sb_norm_w, w_out, norm_mem_w, norm_memkv_w, w_mq, w_mk, w_mv, w_mo, norm_ffn_w, w_up, conv_ffn_w, conv_ffn_b, w_down, norm_final_w):
    B, S, D = x.shape
    T = B * S
    depth = w_in.shape[0]
    H = SSD_HEADS
    DI = ssd_norm_w.shape[1]
    C = conv_ssd_w.shape[2]
    W = sb_norm_w.shape[1]
    o1, o2 = DI, DI + C
    o3 = o2 + H
    o4, o5 = o3 + W, o3 + 2 * W

    x2d = x.reshape(T, D)
    mem2d = mem.reshape(B * mem.shape[1], D)
    for l in range(depth):
        bf = lambda a: a.astype(_BF)
        z, xbc, dt_raw, q, k, v = _in_proj(x2d, norm_mix_w[l][None], bf(w_in[l]), (o1, o2, o3, o4, o5, o5 + W),
                                           tm=min(IN_PROJ_ROWS, S))
        y_ssd = _ssd(z, xbc, dt_raw, conv_ssd_w[l], conv_ssd_b[l][None], _pad_lanes(dt_bias[l][None]),
                     _pad_lanes(a_log[l][None]), jnp.repeat(d_skip[l], DI // H)[None], ssd_norm_w[l][None],
                     B=B, S=S)
        y_sb = _sb_attention(q, k, v, sb_norm_w[l][None], B=B, S=S)
        mk, mv = _mem_kv(mem2d, norm_memkv_w[l][None], bf(w_mk[l]), bf(w_mv[l]))
        x2d = _mix_mem(x2d, y_ssd, y_sb, bf(w_out[l][:DI]), bf(w_out[l][DI:]), norm_mem_w[l][None],
                       bf(w_mq[l]), mk, mv, bf(w_mo[l]), B=B, S=S, tm=min(MIX_ROWS, S))
        x2d = _ffn(x2d, norm_ffn_w[l][None], bf(w_up[l]), conv_ffn_w[l], conv_ffn_b[l][None], bf(w_down[l]),
                   norm_final_w[None], B=B, S=S, tm=min(FFN_ROWS, S), final_norm=(l == depth - 1))
    return x2d.reshape(B, S, D)
```

```python
import functools
import math

import jax
import jax.numpy as jnp
from jax import lax
from jax.experimental import pallas as pl
from jax.experimental.pallas import tpu as pltpu

EPS = 1e-6
LANES = 128
SUBLANES = 8
VMEM_LIMIT = 56 * 1024 * 1024

SSD_HEADS = 16
SSD_GROUPS = 2
SSD_STATE = 128
SSD_CONV = 4
SSD_CHUNK = 128
SSD_CHUNKS_PER_STEP = 4
SB_HEAD_DIM = 64
SB_TILE = 128
SB_PAIRS_PER_STEP = 8
SB_TOP_ROWS = 48
SB_QTILES_PER_STEP = 8
MEM_HEADS = 4
IN_PROJ_ROWS = 512
MIX_ROWS = 1024
FFN_ROWS = 512
FFN_CONV = 3
FFN_CHUNK = 768
FFN_HALO = 16

SB_SKIP_LOG = -110.0

_BF = jnp.bfloat16
_F32 = jnp.float32


def _dot(a, b):
    return jnp.dot(a, b, preferred_element_type=_F32)


def _dot_nt(a, b):
    return lax.dot_general(a, b, (((1,), (1,)), ((), ())), preferred_element_type=_F32)


def _rms(x, w):
    return x * lax.rsqrt(jnp.mean(x * x, axis=-1, keepdims=True) + EPS) * w


def _softplus(x):
    return jnp.maximum(x, 0.0) + jnp.log(1.0 + jnp.exp(-jnp.abs(x)))


def _silu(x):
    return x * jax.nn.sigmoid(x)


def _params(*sem):
    return pltpu.CompilerParams(dimension_semantics=sem, vmem_limit_bytes=VMEM_LIMIT)


def _const_spec(shape):
    nd = len(shape)
    return pl.BlockSpec(shape, lambda *_: (0,) * nd, pipeline_mode=pl.Buffered(1))


def _inproj_kernel(x_ref, nw_ref, w_ref, z_ref, xbc_ref, dt_ref, q_ref, k_ref, v_ref,
                   wq_ref, wk_ref, wv_ref, *, q_scale, offs):
    o1, o2, o3, o4, o5, o6 = offs

    @pl.when(pl.program_id(0) == 0)
    def _():
        wq_ref[...] = w_ref[:, o3:o4]
        wk_ref[...] = w_ref[:, o4:o5]
        wv_ref[...] = w_ref[:, o5:o6]

    hb = _rms(x_ref[...], nw_ref[...]).astype(_BF)
    z_ref[...] = _dot(hb, w_ref[:, 0:o1])
    xbc_ref[...] = _dot(hb, w_ref[:, o1:o2])
    dt_ref[...] = _dot(hb, w_ref[:, o2:o2 + LANES])
    q_ref[...] = (_dot(hb, wq_ref[...]) * q_scale).astype(_BF)
    k_ref[...] = _dot(hb, wk_ref[...]).astype(_BF)
    v_ref[...] = _dot(hb, wv_ref[...]).astype(_BF)


def _in_proj(x2d, nw, w, offs, *, tm):
    T, D = x2d.shape
    o1, o2, o3, o4, o5, o6 = offs
    assert o1 % LANES == 0 and o2 % LANES == 0 and o2 + LANES <= o6
    row = lambda n: pl.BlockSpec((tm, n), lambda i: (i, 0))
    widths = (o1, o2 - o1, LANES, o4 - o3, o5 - o4, o6 - o5)
    out_dtypes = (_F32, _F32, _F32, _BF, _BF, _BF)
    return pl.pallas_call(
        functools.partial(_inproj_kernel, q_scale=1.0 / math.sqrt(SB_HEAD_DIM), offs=offs),
        grid=(T // tm,),
        in_specs=[row(D), _const_spec(nw.shape), _const_spec(w.shape)],
        out_specs=[row(n) for n in widths],
        out_shape=[jax.ShapeDtypeStruct((T, n), dt) for n, dt in zip(widths, out_dtypes)],
        scratch_shapes=[pltpu.VMEM((D, n), _BF) for n in widths[3:]],
        compiler_params=_params("arbitrary"),
        name="in_proj",
    )(x2d, nw, w)


def _split3(v):
    v1 = v.astype(_BF)
    r1 = v - v1.astype(_F32)
    v2 = r1.astype(_BF)
    v3 = (r1 - v2.astype(_F32)).astype(_BF)
    return v1, v2, v3


def _ssd_kernel(z_ref, xbc_ref, dt_ref, cw_ref, cb_ref, dtb_ref, alog_ref, dsk_ref, nw_ref,
                y_ref, tail_ref, state_ref, *, L, H, P, G, N, nsub):
    DI = H * P
    HG = H // G
    GW = DI // G
    R = nsub * L
    c = pl.program_id(1)

    @pl.when(c == 0)
    def _():
        tail_ref[...] = jnp.zeros_like(tail_ref)
        state_ref[...] = jnp.zeros_like(state_ref)

    xbc = xbc_ref[...]
    ext = jnp.concatenate([tail_ref[...], xbc], axis=0)
    tail_ref[...] = xbc[R - SUBLANES:, :]
    conv = cb_ref[...] + cw_ref[SSD_CONV - 1:SSD_CONV, :] * xbc
    for kk in range(1, SSD_CONV):
        conv = conv + cw_ref[SSD_CONV - 1 - kk:SSD_CONV - kk, :] * pltpu.roll(ext, kk, 0)[SUBLANES:]
    act_all = _silu(conv)
    dt_all = _softplus(dt_ref[...] + dtb_ref[...])
    a_all = dt_all * (-jnp.exp(alog_ref[...]))

    ri = lax.broadcasted_iota(jnp.int32, (L, L), 0)
    ci = lax.broadcasted_iota(jnp.int32, (L, L), 1)
    tril = ri >= ci
    tril_bf = jnp.where(tril, 1.0, 0.0).astype(_BF)
    lane = lax.broadcasted_iota(jnp.int32, (1, 2 * P), 1)
    lo = lane < P

    for sc in range(nsub):
        rows = slice(sc * L, (sc + 1) * L)
        act = act_all[rows]
        xs = act[:, :DI]
        Bm = act[:, DI:DI + G * N]
        Cm = act[:, DI + G * N:]
        dt = dt_all[rows]
        a1, a2, a3 = _split3(a_all[rows])
        a_cs = _dot(tril_bf, a1) + _dot(tril_bf, a2) + _dot(tril_bf, a3)
        a_last = a_cs[L - 1:L, :]
        a_cs_t = a_cs.T
        dt_t = dt.T
        w_col = jnp.exp(a_last - a_cs) * dt
        e_col = jnp.exp(a_cs)
        c_dec = jnp.exp(a_last)

        for g in range(G):
            B_g = Bm[:, g * N:(g + 1) * N]
            C_g = Cm[:, g * N:(g + 1) * N].astype(_BF)
            cb = _dot_nt(C_g, B_g.astype(_BF))
            b_t = B_g.T.astype(_BF)
            ys = []
            for pr in range(HG // 2):
                h0 = g * HG + 2 * pr
                h1 = h0 + 1
                cols = slice(h0 * P, (h0 + 2) * P)
                xs2 = xs[:, cols]
                pick = lambda v: jnp.where(lo, v[:, h0:h0 + 1], v[:, h1:h1 + 1])
                prev2 = state_ref[:, cols]
                y_off = _dot(C_g, prev2.astype(_BF)) * pick(e_col)
                ms = []
                for hh in (h0, h1):
                    seg = a_cs[:, hh:hh + 1] - a_cs_t[hh:hh + 1, :]
                    lmat = jnp.exp(jnp.where(tril, seg, -jnp.inf))
                    ms.append((cb * lmat * dt_t[hh:hh + 1, :]).astype(_BF))
                xs_lo = jnp.where(lo, xs2, 0.0).astype(_BF)
                xs_hi = jnp.where(lo, 0.0, xs2).astype(_BF)
                y_diag = _dot(jnp.concatenate(ms, axis=1), jnp.concatenate([xs_lo, xs_hi], axis=0))
                new_state = _dot(b_t, (xs2 * pick(w_col)).astype(_BF))
                state_ref[:, cols] = prev2 * pick(c_dec) + new_state
                y2 = y_diag + y_off + xs2 * dsk_ref[:, cols]
                ys.append(y2 * _silu(z_ref[rows, cols]))
            yg = jnp.concatenate(ys, axis=1)
            y_ref[rows, g * GW:(g + 1) * GW] = _rms(yg, nw_ref[:, g * GW:(g + 1) * GW]).astype(_BF)


def _ssd(z, xbc, dt_raw, cw, cb, dtb, alog, dsk, nw, *, B, S):
    T, DI = z.shape
    C = xbc.shape[1]
    L = SSD_CHUNK
    H, G, N = SSD_HEADS, SSD_GROUPS, SSD_STATE
    P = DI // H
    assert 2 * P == LANES and H % (2 * G) == 0 and S % L == 0
    nsub = math.gcd(SSD_CHUNKS_PER_STEP, S // L)
    nc = S // (L * nsub)
    row = lambda n: pl.BlockSpec((nsub * L, n), lambda b, c: (b * nc + c, 0))
    consts = (cw, cb, dtb, alog, dsk, nw)
    return pl.pallas_call(
        functools.partial(_ssd_kernel, L=L, H=H, P=P, G=G, N=N, nsub=nsub),
        grid=(B, nc),
        in_specs=[row(DI), row(C), row(LANES)] + [_const_spec(a.shape) for a in consts],
        out_specs=row(DI),
        out_shape=jax.ShapeDtypeStruct((T, DI), _BF),
        scratch_shapes=[pltpu.VMEM((SUBLANES, C), _F32), pltpu.VMEM((N, DI), _F32)],
        compiler_params=_params("arbitrary", "arbitrary"),
        name="ssd",
    )(z, xbc, dt_raw, *consts)


def _sb_kernel(q_ref, k_ref, v_ref, nw_ref, o_ref, acc_ref, r_ref, z_ref, p_ref, *, tq, hd, npp, nqb):
    lane = lax.broadcasted_iota(jnp.int32, (1, 2 * hd), 1)
    lo = lane < hd
    nh = 2 * npp
    ri = lax.broadcasted_iota(jnp.int32, (tq, tq), 0)
    ci = lax.broadcasted_iota(jnp.int32, (tq, tq), 1)
    tri = jnp.where(ri >= ci, 1.0, 0.0).astype(_BF)
    tri2 = jnp.concatenate([tri, tri], axis=0)
    rs = lax.broadcasted_iota(jnp.int32, (nh * tq, tq), 0) & (tq - 1)
    cs = lax.broadcasted_iota(jnp.int32, (nh * tq, tq), 1)
    causal = cs < rs
    first_tile = pl.program_id(2) * nqb

    def key_start(j):
        return pl.multiple_of(jnp.maximum(j, 0) * tq, tq)

    def weights(mode, z=None):
        nr = SB_TOP_ROWS if mode == "top" else tq

        def head_rows(ref):
            if mode != "top":
                return ref[...]
            return jnp.concatenate([ref[h * tq:h * tq + nr, :] for h in range(nh)], axis=0)

        if z is None:
            z = head_rows(z_ref)
        if mode == "diag":
            z = jnp.where(causal, z, -jnp.inf)
        lnb = -_softplus(z)
        r = head_rows(r_ref)
        one = jnp.int32(1)
        if mode == "diag":
            live_top, live_rest = one, one
        else:
            tot = r[:, 0:1] + jnp.sum(lnb, axis=-1, keepdims=True)
            if mode == "top":
                live_top, live_rest = (jnp.max(tot) > SB_SKIP_LOG).astype(jnp.int32), jnp.int32(0)
            else:
                pick = lambda lo_row, hi_row: jnp.concatenate(
                    [tot[h * tq + lo_row:h * tq + hi_row] for h in range(nh)], axis=0)
                live_top = (jnp.max(pick(0, SB_TOP_ROWS)) > SB_SKIP_LOG).astype(jnp.int32)
                live_rest = (jnp.max(pick(SB_TOP_ROWS, tq)) > SB_SKIP_LOG).astype(jnp.int32)
        hi = lnb.astype(_BF)
        lw = (lnb - hi.astype(_F32)).astype(_BF)
        cum = _dot(jnp.concatenate([hi, lw], axis=1), tri2)
        a = jnp.exp(z + cum + r)
        r_new = r + cum[:, 0:1]
        if mode == "top":
            for h in range(nh):
                r_ref[h * tq:h * tq + nr, :] = r_new[h * nr:(h + 1) * nr]
        else:
            r_ref[...] = r_new
        a = a.astype(_BF)
        for p in range(npp):
            p_ref[p, 0:nr, :] = jnp.concatenate(
                [a[2 * p * nr:(2 * p + 1) * nr], a[(2 * p + 1) * nr:(2 * p + 2) * nr]], axis=1)
            if mode == "top":
                p_ref[p, nr:tq, :] = jnp.zeros((tq - nr, 2 * tq), _BF)
        return live_top, live_rest

    def apply_weights(j):
        start = key_start(j)
        for p in range(npp):
            cols = slice(p * LANES, (p + 1) * LANES)
            v2 = v_ref[pl.ds(start, tq), cols]
            zv = jnp.zeros_like(v2)
            vv = jnp.concatenate([jnp.where(lo, v2, zv), jnp.where(lo, zv, v2)], axis=0)
            acc_ref[:, cols] += _dot(p_ref[p], vv)

    def finish(qb):
        rows = pl.ds(pl.multiple_of(qb * tq, tq), tq)
        for p in range(npp):
            cols = slice(p * LANES, (p + 1) * LANES)
            acc = acc_ref[:, cols]
            sq = acc * acc
            s_lo = jnp.sum(jnp.where(lo, sq, 0.0), axis=-1, keepdims=True)
            s_hi = jnp.sum(jnp.where(lo, 0.0, sq), axis=-1, keepdims=True)
            inv = lax.rsqrt(jnp.where(lo, s_lo, s_hi) * (1.0 / hd) + EPS)
            o_ref[rows, cols] = (acc * inv * nw_ref[:, cols]).astype(_BF)

    acc_ref[...] = jnp.zeros_like(acc_ref)
    p_ref[...] = jnp.zeros_like(p_ref)

    def query_tile(qb, pending):
        i = first_tile + qb
        rows = pl.ds(pl.multiple_of(qb * tq, tq), tq)
        qcat = []
        for p in range(npp):
            q2 = q_ref[rows, p * LANES:(p + 1) * LANES]
            zq = jnp.zeros_like(q2)
            qcat.append(jnp.concatenate([jnp.where(lo, q2, zq), jnp.where(lo, zq, q2)], axis=0))

        def scores(j):
            start = key_start(j)
            return jnp.concatenate(
                [_dot_nt(qcat[p], k_ref[pl.ds(start, tq), p * LANES:(p + 1) * LANES]) for p in range(npp)],
                axis=0)

        z_first = scores(i)
        z_left = jnp.where(i > 0, scores(i - 1), -jnp.inf)
        z_next = scores(i - 2)
        apply_weights(pending)
        finish(jnp.maximum(qb - 1, 0))
        acc_ref[...] = jnp.zeros_like(acc_ref)
        r_ref[...] = jnp.zeros_like(r_ref)
        weights("diag", z_first)
        apply_weights(i)
        live_top, live_rest = weights("full", z_left)
        z_ref[...] = z_next

        def step(mode):
            def body(carry):
                j = carry[0]
                apply_weights(j + 1)
                z_next = scores(j - 1)
                live = weights(mode)
                z_ref[...] = z_next
                return (j - 1,) + (live if mode == "full" else live[:1])
            return body

        j_mid, live_top, _ = lax.while_loop(
            lambda c: jnp.logical_and(c[0] >= 0, c[2] > 0), step("full"), (i - 2, live_top, live_rest))
        j_end, _ = lax.while_loop(
            lambda c: jnp.logical_and(c[0] >= 0, c[1] > 0), step("top"), (j_mid, live_top))
        return j_end + 1

    pending = lax.fori_loop(0, nqb, query_tile, jnp.int32(0))
    apply_weights(pending)
    finish(nqb - 1)


def _sb_attention(q, k, v, nw, *, B, S):
    T, W = q.shape
    hd = SB_HEAD_DIM
    tq = SB_TILE
    npp = SB_PAIRS_PER_STEP
    cw = npp * LANES
    assert 2 * hd == LANES and tq == LANES and S % tq == 0 and W % cw == 0
    nqb = math.gcd(SB_QTILES_PER_STEP, S // tq)
    nq = S // (tq * nqb)
    k3 = k.reshape(B, S, W)
    v3 = v.reshape(B, S, W)
    return pl.pallas_call(
        functools.partial(_sb_kernel, tq=tq, hd=hd, npp=npp, nqb=nqb),
        grid=(B, W // cw, nq),
        in_specs=[
            pl.BlockSpec((nqb * tq, cw), lambda b, p, i: (b * nq + i, p)),
            pl.BlockSpec((None, S, cw), lambda b, p, i: (b, 0, p), pipeline_mode=pl.Buffered(1)),
            pl.BlockSpec((None, S, cw), lambda b, p, i: (b, 0, p), pipeline_mode=pl.Buffered(1)),
            pl.BlockSpec((1, cw), lambda b, p, i: (0, p)),
        ],
        out_specs=pl.BlockSpec((nqb * tq, cw), lambda b, p, i: (b * nq + i, p)),
        out_shape=jax.ShapeDtypeStruct((T, W), _BF),
        scratch_shapes=[pltpu.VMEM((tq, cw), _F32), pltpu.VMEM((2 * npp * tq, tq), _F32),
                        pltpu.VMEM((2 * npp * tq, tq), _F32), pltpu.VMEM((npp, tq, 2 * tq), _BF)],
        compiler_params=_params("arbitrary", "arbitrary", "arbitrary"),
        name="sb_attn",
    )(q, k3, v3, nw)


def _mixmem_kernel(x_ref, ya_ref, yb_ref, wo1_ref, wo2_ref, nw_ref, wq_ref, mem_ref, mnw_ref, wmk_ref,
                   wmv_ref, wo_ref, o_ref, mk_ref, mv_ref, *, heads, q_scale):
    @pl.when(pl.program_id(1) == 0)
    def _():
        mb = _rms(mem_ref[...], mnw_ref[...]).astype(_BF)
        mk_ref[...] = _dot(mb, wmk_ref[...]).astype(_BF)
        mv_ref[...] = _dot(mb, wmv_ref[...]).astype(_BF)

    x1 = x_ref[...] + _dot(ya_ref[...], wo1_ref[...]) + _dot(yb_ref[...], wo2_ref[...])
    hb = _rms(x1, nw_ref[...]).astype(_BF)
    q = (_dot(hb, wq_ref[...]) * q_scale).astype(_BF)
    D = q.shape[1]
    hd = D // heads
    outs = []
    for h in range(heads):
        cols = slice(h * hd, (h + 1) * hd)
        sc = _dot_nt(q[:, cols], mk_ref[:, cols])
        sc = sc - jnp.max(sc, axis=-1, keepdims=True)
        e = jnp.exp(sc)
        p = e / jnp.sum(e, axis=-1, keepdims=True)
        outs.append(_dot(p.astype(_BF), mv_ref[:, cols]).astype(_BF))
    o = jnp.concatenate(outs, axis=1)
    o_ref[...] = x1 + _dot(o, wo_ref[...])


def _mix_mem(x2d, ya, yb, wo1, wo2, nw, wq, mem2d, mnw, wmk, wmv, wo, *, B, S, tm):
    T, D = x2d.shape
    M = mem2d.shape[0] // B
    nt = S // tm
    row = pl.BlockSpec((tm, D), lambda b, i: (b * nt + i, 0))
    memspec = pl.BlockSpec((M, D), lambda b, i: (b, 0))
    hd = D // MEM_HEADS
    return pl.pallas_call(
        functools.partial(_mixmem_kernel, heads=MEM_HEADS, q_scale=1.0 / math.sqrt(hd)),
        grid=(B, nt),
        in_specs=[row, row, row, _const_spec(wo1.shape), _const_spec(wo2.shape), _const_spec(nw.shape),
                  _const_spec(wq.shape), memspec, _const_spec(mnw.shape), _const_spec(wmk.shape),
                  _const_spec(wmv.shape), _const_spec(wo.shape)],
        out_specs=row,
        out_shape=jax.ShapeDtypeStruct((T, D), _F32),
        scratch_shapes=[pltpu.VMEM((M, D), _BF), pltpu.VMEM((M, D), _BF)],
        compiler_params=_params("arbitrary", "arbitrary"),
        name="mix_mem",
    )(x2d, ya, yb, wo1, wo2, nw, wq, mem2d, mnw, wmk, wmv, wo)


def _ffn_kernel(x_ref, halo_ref, nw_ref, wup_ref, cw_ref, cb_ref, wdn_ref, fw_ref,
                o_ref, acc_ref, h_ref, ua_ref, ub_ref, aa_ref, ab_ref, *, tm, fc, dff, final_norm):
    i = pl.program_id(1)
    x = x_ref[...]
    keep = (i > 0).astype(_F32)
    h_ref[...] = jnp.concatenate(
        [_rms(halo_ref[...], nw_ref[...]) * keep, _rms(x, nw_ref[...])], axis=0).astype(_BF)
    acc_ref[...] = jnp.zeros_like(acc_ref)

    ubuf = (ua_ref, ub_ref)
    abuf = (aa_ref, ab_ref)
    starts = list(range(0, dff, fc))
    widths = [min(fc, dff - s0) for s0 in starts]
    nchunk = len(starts)

    def cols(c, base):
        return slice(base + starts[c], base + starts[c] + widths[c])

    def up(c):
        w = widths[c]
        ubuf[c % 2][0, :, :w] = _dot(h_ref[...], wup_ref[:, cols(c, 0)])
        ubuf[c % 2][1, :, :w] = _dot(h_ref[...], wup_ref[:, cols(c, dff)])

    def conv(u, cw, cb):
        out = cb + cw[FFN_CONV - 1:FFN_CONV, :] * u
        for kk in range(1, FFN_CONV):
            out = out + cw[FFN_CONV - 1 - kk:FFN_CONV - kk, :] * pltpu.roll(u, kk, 0)
        return out[FFN_HALO:]

    def gate(c):
        w = widths[c]
        g = conv(ubuf[c % 2][0, :, :w], cw_ref[:, cols(c, 0)], cb_ref[:, cols(c, 0)])
        val = conv(ubuf[c % 2][1, :, :w], cw_ref[:, cols(c, dff)], cb_ref[:, cols(c, dff)])
        abuf[c % 2][:, :w] = (_silu(g) * val).astype(_BF)

    def down(c):
        acc_ref[...] += _dot(abuf[c % 2][:, :widths[c]], wdn_ref[cols(c, 0), :])

    up(0)
    if nchunk > 1:
        up(1)
    gate(0)
    for c in range(nchunk):
        down(c)
        if c + 1 < nchunk:
            gate(c + 1)
        if c + 2 < nchunk:
            up(c + 2)
    y = x + acc_ref[...]
    o_ref[...] = _rms(y, fw_ref[...]) if final_norm else y


def _ffn(x2d, nw, wup, cw, cb, wdn, fw, *, B, S, tm, final_norm):
    T, D = x2d.shape
    dff = wdn.shape[0]
    fc = FFN_CHUNK
    assert dff % LANES == 0 and tm % FFN_HALO == 0
    nt = S // tm
    hb = tm // FFN_HALO
    row = pl.BlockSpec((tm, D), lambda b, i: (b * nt + i, 0))
    halo = pl.BlockSpec((FFN_HALO, D), lambda b, i: (jnp.maximum((b * nt + i) * hb - 1, 0), 0))
    return pl.pallas_call(
        functools.partial(_ffn_kernel, tm=tm, fc=fc, dff=dff, final_norm=final_norm),
        grid=(B, nt),
        in_specs=[row, halo] + [_const_spec(a.shape) for a in (nw, wup, cw, cb, wdn, fw)],
        out_specs=row,
        out_shape=jax.ShapeDtypeStruct((T, D), _F32),
        scratch_shapes=[pltpu.VMEM((tm, D), _F32),
                        pltpu.VMEM((tm + FFN_HALO, D), _BF),
                        pltpu.VMEM((2, tm + FFN_HALO, fc), _F32),
                        pltpu.VMEM((2, tm + FFN_HALO, fc), _F32),
                        pltpu.VMEM((tm, fc), _BF),
                        pltpu.VMEM((tm, fc), _BF)],
        compiler_params=_params("arbitrary", "arbitrary"),
        name="ffn",
    )(x2d, x2d, nw, wup, cw, cb, wdn, fw)


def _pad_lanes(a):
    return jnp.pad(a, ((0, 0), (0, LANES - a.shape[1])))


def kernel(x, mem, norm_mix_w, w_in, conv_ssd_w, conv_ssd_b, dt_bias, a_log, d_skip, ssd_norm_w, sb_norm_w, w_out, norm_mem_w, norm_memkv_w, w_mq, w_mk, w_mv, w_mo, norm_ffn_w, w_up, conv_ffn_w, conv_ffn_b, w_down, norm_final_w):
    B, S, D = x.shape
    T = B * S
    depth = w_in.shape[0]
    H = SSD_HEADS
    DI = ssd_norm_w.shape[1]
    C = conv_ssd_w.shape[2]
    W = sb_norm_w.shape[1]
    o1, o2 = DI, DI + C
    o3 = o2 + H
    o4, o5 = o3 + W, o3 + 2 * W

    x2d = x.reshape(T, D)
    mem2d = mem.reshape(B * mem.shape[1], D)
    for l in range(depth):
        bf = lambda a: a.astype(_BF)
        z, xbc, dt_raw, q, k, v = _in_proj(x2d, norm_mix_w[l][None], bf(w_in[l]), (o1, o2, o3, o4, o5, o5 + W),
                                           tm=min(IN_PROJ_ROWS, S))
        y_ssd = _ssd(z, xbc, dt_raw, conv_ssd_w[l], conv_ssd_b[l][None], _pad_lanes(dt_bias[l][None]),
                     _pad_lanes(a_log[l][None]), jnp.repeat(d_skip[l], DI // H)[None], ssd_norm_w[l][None],
                     B=B, S=S)
        y_sb = _sb_attention(q, k, v, sb_norm_w[l][None], B=B, S=S)
        x2d = _mix_mem(x2d, y_ssd, y_sb, bf(w_out[l][:DI]), bf(w_out[l][DI:]), norm_mem_w[l][None],
                       bf(w_mq[l]), mem2d, norm_memkv_w[l][None], bf(w_mk[l]), bf(w_mv[l]), bf(w_mo[l]),
                       B=B, S=S, tm=min(MIX_ROWS, S))
        x2d = _ffn(x2d, norm_ffn_w[l][None], bf(w_up[l]), conv_ffn_w[l], conv_ffn_b[l][None], bf(w_down[l]),
                   norm_final_w[None], B=B, S=S, tm=min(FFN_ROWS, S), final_norm=(l == depth - 1))
    return x2d.reshape(B, S, D)
```

```python
import functools
import math

import jax
import jax.numpy as jnp
from jax import lax
from jax.experimental import pallas as pl
from jax.experimental.pallas import tpu as pltpu

EPS = 1e-6
LANES = 128
SUBLANES = 8
VMEM_LIMIT = 56 * 1024 * 1024

SSD_HEADS = 16
SSD_GROUPS = 2
SSD_STATE = 128
SSD_CONV = 4
SSD_CHUNK = 128
SSD_CHUNKS_PER_STEP = 4
SB_HEAD_DIM = 64
SB_TILE = 128
SB_PAIRS_PER_STEP = 8
SB_TOP_ROWS = 48
SB_QTILES_PER_STEP = 8
MEM_HEADS = 4
IN_PROJ_ROWS = 512
MIX_ROWS = 1024
FFN_ROWS = 512
FFN_CONV = 3
FFN_CHUNK = 768
FFN_HALO = 16

SB_SKIP_LOG = -110.0

_BF = jnp.bfloat16
_F32 = jnp.float32


def _dot(a, b):
    return jnp.dot(a, b, preferred_element_type=_F32)


def _dot_nt(a, b):
    return lax.dot_general(a, b, (((1,), (1,)), ((), ())), preferred_element_type=_F32)


def _rms(x, w):
    return x * lax.rsqrt(jnp.mean(x * x, axis=-1, keepdims=True) + EPS) * w


def _softplus(x):
    return jnp.maximum(x, 0.0) + jnp.log(1.0 + jnp.exp(-jnp.abs(x)))


def _silu(x):
    return x * jax.nn.sigmoid(x)


def _params(*sem):
    return pltpu.CompilerParams(dimension_semantics=sem, vmem_limit_bytes=VMEM_LIMIT)


def _const_spec(shape):
    nd = len(shape)
    return pl.BlockSpec(shape, lambda *_: (0,) * nd, pipeline_mode=pl.Buffered(1))


def _inproj_kernel(x_ref, nw_ref, w_ref, z_ref, xbc_ref, dt_ref, q_ref, k_ref, v_ref,
                   wq_ref, wk_ref, wv_ref, *, q_scale, offs):
    o1, o2, o3, o4, o5, o6 = offs

    @pl.when(pl.program_id(0) == 0)
    def _():
        wq_ref[...] = w_ref[:, o3:o4]
        wk_ref[...] = w_ref[:, o4:o5]
        wv_ref[...] = w_ref[:, o5:o6]

    hb = _rms(x_ref[...], nw_ref[...]).astype(_BF)
    z_ref[...] = _dot(hb, w_ref[:, 0:o1])
    xbc_ref[...] = _dot(hb, w_ref[:, o1:o2])
    dt_ref[...] = _dot(hb, w_ref[:, o2:o2 + LANES])
    q_ref[...] = (_dot(hb, wq_ref[...]) * q_scale).astype(_BF)
    k_ref[...] = _dot(hb, wk_ref[...]).astype(_BF)
    v_ref[...] = _dot(hb, wv_ref[...]).astype(_BF)


def _in_proj(x2d, nw, w, offs, *, tm):
    T, D = x2d.shape
    o1, o2, o3, o4, o5, o6 = offs
    assert o1 % LANES == 0 and o2 % LANES == 0 and o2 + LANES <= o6
    row = lambda n: pl.BlockSpec((tm, n), lambda i: (i, 0))
    widths = (o1, o2 - o1, LANES, o4 - o3, o5 - o4, o6 - o5)
    out_dtypes = (_F32, _F32, _F32, _BF, _BF, _BF)
    return pl.pallas_call(
        functools.partial(_inproj_kernel, q_scale=1.0 / math.sqrt(SB_HEAD_DIM), offs=offs),
        grid=(T // tm,),
        in_specs=[row(D), _const_spec(nw.shape), _const_spec(w.shape)],
        out_specs=[row(n) for n in widths],
        out_shape=[jax.ShapeDtypeStruct((T, n), dt) for n, dt in zip(widths, out_dtypes)],
        scratch_shapes=[pltpu.VMEM((D, n), _BF) for n in widths[3:]],
        compiler_params=_params("arbitrary"),
        name="in_proj",
    )(x2d, nw, w)


def _split3(v):
    v1 = v.astype(_BF)
    r1 = v - v1.astype(_F32)
    v2 = r1.astype(_BF)
    v3 = (r1 - v2.astype(_F32)).astype(_BF)
    return v1, v2, v3


def _ssd_kernel(z_ref, xbc_ref, dt_ref, cw_ref, cb_ref, dtb_ref, alog_ref, dsk_ref, nw_ref,
                y_ref, tail_ref, state_ref, *, L, H, P, G, N, nsub):
    DI = H * P
    HG = H // G
    GW = DI // G
    R = nsub * L
    c = pl.program_id(1)

    @pl.when(c == 0)
    def _():
        tail_ref[...] = jnp.zeros_like(tail_ref)
        state_ref[...] = jnp.zeros_like(state_ref)

    xbc = xbc_ref[...]
    ext = jnp.concatenate([tail_ref[...], xbc], axis=0)
    tail_ref[...] = xbc[R - SUBLANES:, :]
    conv = cb_ref[...] + cw_ref[SSD_CONV - 1:SSD_CONV, :] * xbc
    for kk in range(1, SSD_CONV):
        conv = conv + cw_ref[SSD_CONV - 1 - kk:SSD_CONV - kk, :] * pltpu.roll(ext, kk, 0)[SUBLANES:]
    act_all = _silu(conv)
    dt_all = _softplus(dt_ref[...] + dtb_ref[...])
    a_all = dt_all * (-jnp.exp(alog_ref[...]))

    ri = lax.broadcasted_iota(jnp.int32, (L, L), 0)
    ci = lax.broadcasted_iota(jnp.int32, (L, L), 1)
    tril = ri >= ci
    tril_bf = jnp.where(tril, 1.0, 0.0).astype(_BF)
    lane = lax.broadcasted_iota(jnp.int32, (1, 2 * P), 1)
    lo = lane < P

    for sc in range(nsub):
        rows = slice(sc * L, (sc + 1) * L)
        act = act_all[rows]
        xs = act[:, :DI]
        Bm = act[:, DI:DI + G * N]
        Cm = act[:, DI + G * N:]
        dt = dt_all[rows]
        a1, a2, a3 = _split3(a_all[rows])
        a_cs = _dot(tril_bf, a1) + _dot(tril_bf, a2) + _dot(tril_bf, a3)
        a_last = a_cs[L - 1:L, :]
        a_cs_t = a_cs.T
        dt_t = dt.T
        w_col = jnp.exp(a_last - a_cs) * dt
        e_col = jnp.exp(a_cs)
        c_dec = jnp.exp(a_last)

        for g in range(G):
            B_g = Bm[:, g * N:(g + 1) * N]
            C_g = Cm[:, g * N:(g + 1) * N].astype(_BF)
            cb = _dot_nt(C_g, B_g.astype(_BF))
            b_t = B_g.T.astype(_BF)
            ys = []
            for pr in range(HG // 2):
                h0 = g * HG + 2 * pr
                h1 = h0 + 1
                cols = slice(h0 * P, (h0 + 2) * P)
                xs2 = xs[:, cols]
                pick = lambda v: jnp.where(lo, v[:, h0:h0 + 1], v[:, h1:h1 + 1])
                prev2 = state_ref[:, cols]
                y_off = _dot(C_g, prev2.astype(_BF)) * pick(e_col)
                ms = []
                for hh in (h0, h1):
                    seg = a_cs[:, hh:hh + 1] - a_cs_t[hh:hh + 1, :]
                    lmat = jnp.exp(jnp.where(tril, seg, -jnp.inf))
                    ms.append((cb * lmat * dt_t[hh:hh + 1, :]).astype(_BF))
                xs_lo = jnp.where(lo, xs2, 0.0).astype(_BF)
                xs_hi = jnp.where(lo, 0.0, xs2).astype(_BF)
                y_diag = _dot(jnp.concatenate(ms, axis=1), jnp.concatenate([xs_lo, xs_hi], axis=0))
                new_state = _dot(b_t, (xs2 * pick(w_col)).astype(_BF))
                state_ref[:, cols] = prev2 * pick(c_dec) + new_state
                y2 = y_diag + y_off + xs2 * dsk_ref[:, cols]
                ys.append(y2 * _silu(z_ref[rows, cols]))
            yg = jnp.concatenate(ys, axis=1)
            y_ref[rows, g * GW:(g + 1) * GW] = _rms(yg, nw_ref[:, g * GW:(g + 1) * GW]).astype(_BF)


def _ssd(z, xbc, dt_raw, cw, cb, dtb, alog, dsk, nw, *, B, S):
    T, DI = z.shape
    C = xbc.shape[1]
    L = SSD_CHUNK
    H, G, N = SSD_HEADS, SSD_GROUPS, SSD_STATE
    P = DI // H
    assert 2 * P == LANES and H % (2 * G) == 0 and S % L == 0
    nsub = math.gcd(SSD_CHUNKS_PER_STEP, S // L)
    nc = S // (L * nsub)
    row = lambda n: pl.BlockSpec((nsub * L, n), lambda b, c: (b * nc + c, 0))
    consts = (cw, cb, dtb, alog, dsk, nw)
    return pl.pallas_call(
        functools.partial(_ssd_kernel, L=L, H=H, P=P, G=G, N=N, nsub=nsub),
        grid=(B, nc),
        in_specs=[row(DI), row(C), row(LANES)] + [_const_spec(a.shape) for a in consts],
        out_specs=row(DI),
        out_shape=jax.ShapeDtypeStruct((T, DI), _BF),
        scratch_shapes=[pltpu.VMEM((SUBLANES, C), _F32), pltpu.VMEM((N, DI), _F32)],
        compiler_params=_params("arbitrary", "arbitrary"),
        name="ssd",
    )(z, xbc, dt_raw, *consts)


def _sb_kernel(q_ref, k_ref, v_ref, nw_ref, o_ref, acc_ref, r_ref, z_ref, p_ref, *, tq, hd, npp, nqb):
    lane = lax.broadcasted_iota(jnp.int32, (1, 2 * hd), 1)
    lo = lane < hd
    nh = 2 * npp
    ri = lax.broadcasted_iota(jnp.int32, (tq, tq), 0)
    ci = lax.broadcasted_iota(jnp.int32, (tq, tq), 1)
    tri = jnp.where(ri >= ci, 1.0, 0.0).astype(_BF)
    tri2 = jnp.concatenate([tri, tri], axis=0)
    rs = lax.broadcasted_iota(jnp.int32, (nh * tq, tq), 0) & (tq - 1)
    cs = lax.broadcasted_iota(jnp.int32, (nh * tq, tq), 1)
    causal = cs < rs
    first_tile = pl.program_id(2) * nqb

    def key_start(j):
        return pl.multiple_of(jnp.maximum(j, 0) * tq, tq)

    def weights(mode, z=None):
        nr = SB_TOP_ROWS if mode == "top" else tq

        def head_rows(ref):
            if mode != "top":
                return ref[...]
            return jnp.concatenate([ref[h * tq:h * tq + nr, :] for h in range(nh)], axis=0)

        if z is None:
            z = head_rows(z_ref)
        if mode == "diag":
            z = jnp.where(causal, z, -jnp.inf)
        lnb = -_softplus(z)
        r = head_rows(r_ref)
        one = jnp.int32(1)
        if mode == "diag":
            live_top, live_rest = one, one
        else:
            tot = r[:, 0:1] + jnp.sum(lnb, axis=-1, keepdims=True)
            if mode == "top":
                live_top, live_rest = (jnp.max(tot) > SB_SKIP_LOG).astype(jnp.int32), jnp.int32(0)
            else:
                pick = lambda lo_row, hi_row: jnp.concatenate(
                    [tot[h * tq + lo_row:h * tq + hi_row] for h in range(nh)], axis=0)
                live_top = (jnp.max(pick(0, SB_TOP_ROWS)) > SB_SKIP_LOG).astype(jnp.int32)
                live_rest = (jnp.max(pick(SB_TOP_ROWS, tq)) > SB_SKIP_LOG).astype(jnp.int32)
        hi = lnb.astype(_BF)
        lw = (lnb - hi.astype(_F32)).astype(_BF)
        cum = _dot(jnp.concatenate([hi, lw], axis=1), tri2)
        a = jnp.exp(z + cum + r)
        r_new = r + cum[:, 0:1]
        if mode == "top":
            for h in range(nh):
                r_ref[h * tq:h * tq + nr, :] = r_new[h * nr:(h + 1) * nr]
        else:
            r_ref[...] = r_new
        a = a.astype(_BF)
        for p in range(npp):
            p_ref[p, 0:nr, :] = jnp.concatenate(
                [a[2 * p * nr:(2 * p + 1) * nr], a[(2 * p + 1) * nr:(2 * p + 2) * nr]], axis=1)
            if mode == "top":
                p_ref[p, nr:tq, :] = jnp.zeros((tq - nr, 2 * tq), _BF)
        return live_top, live_rest

    def apply_weights(j):
        start = key_start(j)
        for p in range(npp):
            cols = slice(p * LANES, (p + 1) * LANES)
            v2 = v_ref[pl.ds(start, tq), cols]
            zv = jnp.zeros_like(v2)
            vv = jnp.concatenate([jnp.where(lo, v2, zv), jnp.where(lo, zv, v2)], axis=0)
            acc_ref[:, cols] += _dot(p_ref[p], vv)

    def finish(qb):
        rows = pl.ds(pl.multiple_of(qb * tq, tq), tq)
        for p in range(npp):
            cols = slice(p * LANES, (p + 1) * LANES)
            acc = acc_ref[:, cols]
            sq = acc * acc
            s_lo = jnp.sum(jnp.where(lo, sq, 0.0), axis=-1, keepdims=True)
            s_hi = jnp.sum(jnp.where(lo, 0.0, sq), axis=-1, keepdims=True)
            inv = lax.rsqrt(jnp.where(lo, s_lo, s_hi) * (1.0 / hd) + EPS)
            o_ref[rows, cols] = (acc * inv * nw_ref[:, cols]).astype(_BF)

    acc_ref[...] = jnp.zeros_like(acc_ref)
    p_ref[...] = jnp.zeros_like(p_ref)

    def query_tile(qb, pending):
        i = first_tile + qb
        rows = pl.ds(pl.multiple_of(qb * tq, tq), tq)
        qcat = []
        for p in range(npp):
            q2 = q_ref[rows, p * LANES:(p + 1) * LANES]
            zq = jnp.zeros_like(q2)
            qcat.append(jnp.concatenate([jnp.where(lo, q2, zq), jnp.where(lo, zq, q2)], axis=0))

        def scores(j):
            start = key_start(j)
            return jnp.concatenate(
                [_dot_nt(qcat[p], k_ref[pl.ds(start, tq), p * LANES:(p + 1) * LANES]) for p in range(npp)],
                axis=0)

        z_first = scores(i)
        z_left = jnp.where(i > 0, scores(i - 1), -jnp.inf)
        z_next = scores(i - 2)
        apply_weights(pending)
        finish(jnp.maximum(qb - 1, 0))
        acc_ref[...] = jnp.zeros_like(acc_ref)
        r_ref[...] = jnp.zeros_like(r_ref)
        weights("diag", z_first)
        apply_weights(i)
        live_top, live_rest = weights("full", z_left)
        z_ref[...] = z_next

        def step(mode):
            def body(carry):
                j = carry[0]
                apply_weights(j + 1)
                z_next = scores(j - 1)
                live = weights(mode)
                z_ref[...] = z_next
                return (j - 1,) + (live if mode == "full" else live[:1])
            return body

        j_mid, live_top, _ = lax.while_loop(
            lambda c: jnp.logical_and(c[0] >= 0, c[2] > 0), step("full"), (i - 2, live_top, live_rest))
        j_end, _ = lax.while_loop(
            lambda c: jnp.logical_and(c[0] >= 0, c[1] > 0), step("top"), (j_mid, live_top))
        return j_end + 1

    pending = lax.fori_loop(0, nqb, query_tile, jnp.int32(0))
    apply_weights(pending)
    finish(nqb - 1)


def _sb_attention(q, k, v, nw, *, B, S):
    T, W = q.shape
    hd = SB_HEAD_DIM
    tq = SB_TILE
    npp = SB_PAIRS_PER_STEP
    cw = npp * LANES
    assert 2 * hd == LANES and tq == LANES and S % tq == 0 and W % cw == 0
    nqb = math.gcd(SB_QTILES_PER_STEP, S // tq)
    nq = S // (tq * nqb)
    k3 = k.reshape(B, S, W)
    v3 = v.reshape(B, S, W)
    return pl.pallas_call(
        functools.partial(_sb_kernel, tq=tq, hd=hd, npp=npp, nqb=nqb),
        grid=(B, W // cw, nq),
        in_specs=[
            pl.BlockSpec((nqb * tq, cw), lambda b, p, i: (b * nq + i, p)),
            pl.BlockSpec((None, S, cw), lambda b, p, i: (b, 0, p), pipeline_mode=pl.Buffered(1)),
            pl.BlockSpec((None, S, cw), lambda b, p, i: (b, 0, p), pipeline_mode=pl.Buffered(1)),
            pl.BlockSpec((1, cw), lambda b, p, i: (0, p)),
        ],
        out_specs=pl.BlockSpec((nqb * tq, cw), lambda b, p, i: (b * nq + i, p)),
        out_shape=jax.ShapeDtypeStruct((T, W), _BF),
        scratch_shapes=[pltpu.VMEM((tq, cw), _F32), pltpu.VMEM((2 * npp * tq, tq), _F32),
                        pltpu.VMEM((2 * npp * tq, tq), _F32), pltpu.VMEM((npp, tq, 2 * tq), _BF)],
        compiler_params=_params("parallel", "parallel", "parallel"),
        name="sb_attn",
    )(q, k3, v3, nw)


def _memkv_kernel(m_ref, nw_ref, wk_ref, wv_ref, k_ref, v_ref):
    mb = _rms(m_ref[...], nw_ref[...]).astype(_BF)
    k_ref[...] = _dot(mb, wk_ref[...]).astype(_BF)
    v_ref[...] = _dot(mb, wv_ref[...]).astype(_BF)


def _mem_kv(mem2d, nw, wk, wv):
    R, D = mem2d.shape
    return pl.pallas_call(
        _memkv_kernel,
        grid=(1,),
        in_specs=[_const_spec(a.shape) for a in (mem2d, nw, wk, wv)],
        out_specs=[_const_spec((R, D))] * 2,
        out_shape=[jax.ShapeDtypeStruct((R, D), _BF)] * 2,
        compiler_params=_params("arbitrary"),
        name="mem_kv",
    )(mem2d, nw, wk, wv)


def _mixmem_kernel(x_ref, ya_ref, yb_ref, wo1_ref, wo2_ref, nw_ref, wq_ref, mk_ref, mv_ref, wo_ref,
                   o_ref, *, heads, q_scale):
    x1 = x_ref[...] + _dot(ya_ref[...], wo1_ref[...]) + _dot(yb_ref[...], wo2_ref[...])
    hb = _rms(x1, nw_ref[...]).astype(_BF)
    q = (_dot(hb, wq_ref[...]) * q_scale).astype(_BF)
    D = q.shape[1]
    hd = D // heads
    outs = []
    for h in range(heads):
        cols = slice(h * hd, (h + 1) * hd)
        sc = _dot_nt(q[:, cols], mk_ref[:, cols])
        sc = sc - jnp.max(sc, axis=-1, keepdims=True)
        e = jnp.exp(sc)
        p = e / jnp.sum(e, axis=-1, keepdims=True)
        outs.append(_dot(p.astype(_BF), mv_ref[:, cols]).astype(_BF))
    o = jnp.concatenate(outs, axis=1)
    o_ref[...] = x1 + _dot(o, wo_ref[...])


def _mix_mem(x2d, ya, yb, wo1, wo2, nw, wq, mk, mv, wo, *, B, S, tm):
    T, D = x2d.shape
    M = mk.shape[0] // B
    nt = S // tm
    row = pl.BlockSpec((tm, D), lambda b, i: (b * nt + i, 0))
    memspec = pl.BlockSpec((M, D), lambda b, i: (b, 0))
    hd = D // MEM_HEADS
    return pl.pallas_call(
        functools.partial(_mixmem_kernel, heads=MEM_HEADS, q_scale=1.0 / math.sqrt(hd)),
        grid=(B, nt),
        in_specs=[row, row, row, _const_spec(wo1.shape), _const_spec(wo2.shape), _const_spec(nw.shape),
                  _const_spec(wq.shape), memspec, memspec, _const_spec(wo.shape)],
        out_specs=row,
        out_shape=jax.ShapeDtypeStruct((T, D), _F32),
        compiler_params=_params("parallel", "parallel"),
        name="mix_mem",
    )(x2d, ya, yb, wo1, wo2, nw, wq, mk, mv, wo)


def _ffn_kernel(x_ref, halo_ref, nw_ref, wup_ref, cw_ref, cb_ref, wdn_ref, fw_ref,
                o_ref, acc_ref, h_ref, ua_ref, ub_ref, aa_ref, ab_ref, *, tm, fc, dff, final_norm):
    i = pl.program_id(1)
    x = x_ref[...]
    keep = (i > 0).astype(_F32)
    h_ref[...] = jnp.concatenate(
        [_rms(halo_ref[...], nw_ref[...]) * keep, _rms(x, nw_ref[...])], axis=0).astype(_BF)
    acc_ref[...] = jnp.zeros_like(acc_ref)

    ubuf = (ua_ref, ub_ref)
    abuf = (aa_ref, ab_ref)
    starts = list(range(0, dff, fc))
    widths = [min(fc, dff - s0) for s0 in starts]
    nchunk = len(starts)

    def cols(c, base):
        return slice(base + starts[c], base + starts[c] + widths[c])

    def up(c):
        w = widths[c]
        ubuf[c % 2][0, :, :w] = _dot(h_ref[...], wup_ref[:, cols(c, 0)])
        ubuf[c % 2][1, :, :w] = _dot(h_ref[...], wup_ref[:, cols(c, dff)])

    def conv(u, cw, cb):
        out = cb + cw[FFN_CONV - 1:FFN_CONV, :] * u
        for kk in range(1, FFN_CONV):
            out = out + cw[FFN_CONV - 1 - kk:FFN_CONV - kk, :] * pltpu.roll(u, kk, 0)
        return out[FFN_HALO:]

    def gate(c):
        w = widths[c]
        g = conv(ubuf[c % 2][0, :, :w], cw_ref[:, cols(c, 0)], cb_ref[:, cols(c, 0)])
        val = conv(ubuf[c % 2][1, :, :w], cw_ref[:, cols(c, dff)], cb_ref[:, cols(c, dff)])
        abuf[c % 2][:, :w] = (_silu(g) * val).astype(_BF)

    def down(c):
        acc_ref[...] += _dot(abuf[c % 2][:, :widths[c]], wdn_ref[cols(c, 0), :])

    up(0)
    if nchunk > 1:
        up(1)
    gate(0)
    for c in range(nchunk):
        down(c)
        if c + 1 < nchunk:
            gate(c + 1)
        if c + 2 < nchunk:
            up(c + 2)
    y = x + acc_ref[...]
    o_ref[...] = _rms(y, fw_ref[...]) if final_norm else y


def _ffn(x2d, nw, wup, cw, cb, wdn, fw, *, B, S, tm, final_norm):
    T, D = x2d.shape
    dff = wdn.shape[0]
    fc = FFN_CHUNK
    assert dff % LANES == 0 and tm % FFN_HALO == 0
    nt = S // tm
    hb = tm // FFN_HALO
    row = pl.BlockSpec((tm, D), lambda b, i: (b * nt + i, 0))
    halo = pl.BlockSpec((FFN_HALO, D), lambda b, i: (jnp.maximum((b * nt + i) * hb - 1, 0), 0))
    return pl.pallas_call(
        functools.partial(_ffn_kernel, tm=tm, fc=fc, dff=dff, final_norm=final_norm),
        grid=(B, nt),
        in_specs=[row, halo] + [_const_spec(a.shape) for a in (nw, wup, cw, cb, wdn, fw)],
        out_specs=row,
        out_shape=jax.ShapeDtypeStruct((T, D), _F32),
        scratch_shapes=[pltpu.VMEM((tm, D), _F32),
                        pltpu.VMEM((tm + FFN_HALO, D), _BF),
                        pltpu.VMEM((2, tm + FFN_HALO, fc), _F32),
                        pltpu.VMEM((2, tm + FFN_HALO, fc), _F32),
                        pltpu.VMEM((tm, fc), _BF),
                        pltpu.VMEM((tm, fc), _BF)],
        compiler_params=_params("parallel", "parallel"),
        name="ffn",
    )(x2d, x2d, nw, wup, cw, cb, wdn, fw)


def _pad_lanes(a):
    return jnp.pad(a, ((0, 0), (0, LANES - a.shape[1])))


def kernel(x, mem, norm_mix_w, w_in, conv_ssd_w, conv_ssd_b, dt_bias, a_log, d_skip, ssd_norm_w, sb_norm_w, w_out, norm_mem_w, norm_memkv_w, w_mq, w_mk, w_mv, w_mo, norm_ffn_w, w_up, conv_ffn_w, conv_ffn_b, w_down, norm_final_w):
    B, S, D = x.shape
    T = B * S
    depth = w_in.shape[0]
    H = SSD_HEADS
    DI = ssd_norm_w.shape[1]
    C = conv_ssd_w.shape[2]
    W = sb_norm_w.shape[1]
    o1, o2 = DI, DI + C
    o3 = o2 + H
    o4, o5 = o3 + W, o3 + 2 * W

    x2d = x.reshape(T, D)
    mem2d = mem.reshape(B * mem.shape[1], D)
    for l in range(depth):
        bf = lambda a: a.astype(_BF)
        z, xbc, dt_raw, q, k, v = _in_proj(x2d, norm_mix_w[l][None], bf(w_in[l]), (o1, o2, o3, o4, o5, o5 + W),
                                           tm=min(IN_PROJ_ROWS, S))
        y_ssd = _ssd(z, xbc, dt_raw, conv_ssd_w[l], conv_ssd_b[l][None], _pad_lanes(dt_bias[l][None]),
                     _pad_lanes(a_log[l][None]), jnp.repeat(d_skip[l], DI // H)[None], ssd_norm_w[l][None],
                     B=B, S=S)
        y_sb = _sb_attention(q, k, v, sb_norm_w[l][None], B=B, S=S)
        mk, mv = _mem_kv(mem2d, norm_memkv_w[l][None], bf(w_mk[l]), bf(w_mv[l]))
        x2d = _mix_mem(x2d, y_ssd, y_sb, bf(w_out[l][:DI]), bf(w_out[l][DI:]), norm_mem_w[l][None],
                       bf(w_mq[l]), mk, mv, bf(w_mo[l]), B=B, S=S, tm=min(MIX_ROWS, S))
        x2d = _ffn(x2d, norm_ffn_w[l][None], bf(w_up[l]), conv_ffn_w[l], conv_ffn_b[l][None], bf(w_down[l]),
                   norm_final_w[None], B=B, S=S, tm=min(FFN_ROWS, S), final_norm=(l == depth - 1))
    return x2d.reshape(B, S, D)
```
